```python
import functools
import jax, jax.numpy as jnp
from jax import lax
import numpy as np

D_MODEL = 1024
BATCH = 8
SEQ = 8192
DEPTH = 1
DEC_BATCH = 128
DEC_SEQ = 1
PAST_LEN = 8192
PAGE_SIZE = 128

N_HEADS = 16
N_KV_HEADS = 4
HEAD_DIM = D_MODEL // N_HEADS
HEADS_PER_GROUP = N_HEADS // N_KV_HEADS
Q_DIM = N_HEADS * HEAD_DIM
KV_DIM = N_KV_HEADS * HEAD_DIM
L_CMP = 32
STRIDE = 16
L_SEL = 64
N_SEL = 16
WINDOW = 512
Q_BLOCK = 128
SCALE = HEAD_DIM ** -0.5
NEG = -1e30
D_RNN = D_MODEL
N_RNN_BLOCKS = 16
RNN_BLOCK = D_RNN // N_RNN_BLOCKS
CONV_W = 4
LRU_C = 8.0
D_FF = ((8 * D_MODEL // 3 + 127) // 128) * 128
EPS = 1e-6
_IN_SIZES = (Q_DIM, 2 * KV_DIM, 2 * KV_DIM, 2 * KV_DIM, 3 * N_HEADS, D_RNN, D_RNN, D_MODEL, D_MODEL)
IN_SPLITS = tuple(sum(_IN_SIZES[:i + 1]) for i in range(len(_IN_SIZES) - 1))
D_IN = sum(_IN_SIZES)

kernel_name = 'nsa_rglru_macaron_hybrid_step'


def rmsnorm(x, g):
    x32 = x.astype(jnp.float32)
    y = x32 * lax.rsqrt(jnp.mean(x32 * x32, axis=-1, keepdims=True) + EPS)
    return (y * g.astype(jnp.float32)).astype(x.dtype)


def swiglu(x, w_gate, w_up, w_down):
    return (jax.nn.silu(x @ w_gate) * (x @ w_up)) @ w_down


def half_ffn(x, g_pre, g_post, w_gate, w_up, w_down):
    return x + 0.5 * rmsnorm(swiglu(rmsnorm(x, g_pre), w_gate, w_up, w_down), g_post)


def alibi_slopes():
    h = jnp.arange(1, N_HEADS + 1, dtype=jnp.float32)
    return jnp.exp2(-8.0 * h / N_HEADS).reshape(N_KV_HEADS, HEADS_PER_GROUP)


def compress_blocks(kv, w1, w2, pe):
    n_c = (kv.shape[0] - L_CMP) // STRIDE + 1
    starts = jnp.arange(n_c) * STRIDE
    blocks = kv[starts[:, None] + jnp.arange(L_CMP)]
    hid = jax.nn.gelu(jnp.einsum('clsgd,slde->csge', blocks + pe[None, :, :, None, :], w1))
    out = jnp.einsum('csgd,sde->csge', hid, w2)
    return out[:, 0], out[:, 1], starts + (L_CMP - 1)


def nsa_core(q, gate, t, kc, vc, c_pos, n_blocks, fetch, kw, w_pos):
    f32 = jnp.float32
    n_q = q.shape[0]
    slopes = alibi_slopes()
    qg = q.reshape(n_q, N_KV_HEADS, HEADS_PER_GROUP, HEAD_DIM)
    tf = t.astype(f32)
    s_c = jnp.einsum('qghd,cgd->qghc', qg, kc, preferred_element_type=f32) * SCALE
    s_c = s_c - slopes[None, :, :, None] * (tf[:, None] - c_pos.astype(f32)[None, :])[:, None, None, :]
    ok_c = (c_pos[None, :] <= t[:, None])[:, None, None, :]
    p_c = jax.nn.softmax(jnp.where(ok_c, s_c, NEG), axis=-1) * ok_c
    o_c = jnp.einsum('qghc,cgd->qghd', p_c.astype(vc.dtype), vc)
    cs = jnp.arange(kc.shape[0])[:, None] * STRIDE
    js = jnp.arange(n_blocks)
    overlap = ((cs < (js[None, :] + 1) * L_SEL) & (cs + L_CMP > js[None, :] * L_SEL)).astype(f32)
    imp = jnp.einsum('qghc,cj->qgj', p_c, overlap)
    cur = (t // L_SEL)[:, None, None]
    j3 = js[None, None, :]
    forced = (j3 == 0) | (j3 == cur) | (j3 == cur - 1)
    score = jnp.where(j3 <= cur, jnp.where(forced, jnp.inf, imp), -jnp.inf)
    _, idx = lax.top_k(score, min(N_SEL, n_blocks))
    tok = idx[..., None] * L_SEL + jnp.arange(L_SEL)
    kv_s = fetch(tok)
    s_s = jnp.einsum('qghd,qgnld->qghnl', qg, kv_s[..., 0, :], preferred_element_type=f32) * SCALE
    s_s = s_s - slopes[None, :, :, None, None] * (tf[:, None, None, None] - tok.astype(f32))[:, :, None]
    ok_s = (tok <= t[:, None, None, None])[:, :, None]
    shp = s_s.shape
    p_s = jax.nn.softmax(jnp.where(ok_s, s_s, NEG).reshape(shp[:3] + (-1,)), axis=-1).reshape(shp)
    o_s = jnp.einsum('qghnl,qgnld->qghd', p_s.astype(kv_s.dtype), kv_s[..., 1, :])
    s_w = jnp.einsum('qghd,kgd->qghk', qg, kw[:, 0], preferred_element_type=f32) * SCALE
    dist = t[:, None] - w_pos[None, :]
    s_w = s_w - slopes[None, :, :, None] * dist.astype(f32)[:, None, None, :]
    ok_w = ((dist >= 0) & (dist < WINDOW) & (w_pos >= 0)[None, :])[:, None, None, :]
    p_w = jax.nn.softmax(jnp.where(ok_w, s_w, NEG), axis=-1)
    o_w = jnp.einsum('qghk,kgd->qghd', p_w.astype(kw.dtype), kw[:, 1])
    g = gate.reshape(n_q, 3, N_KV_HEADS, HEADS_PER_GROUP, 1)
    o = g[:, 0] * o_c + g[:, 1] * o_s + g[:, 2] * o_w
    return o.reshape(n_q, N_HEADS, HEAD_DIM)


def nsa_prompt_seq(args, cmp_w1, cmp_w2, cmp_pos):
    q, gate, kv_c, kv_s, kv_w = args
    seq = q.shape[0]
    kc, vc, c_pos = compress_blocks(kv_c, cmp_w1, cmp_w2, cmp_pos)
    n_blocks = -(-seq // L_SEL)
    g_idx = jnp.arange(N_KV_HEADS)[None, :, None, None]

    def fetch(tok):
        return kv_s[tok, :, g_idx]

    kv_w_pad = jnp.pad(kv_w, ((WINDOW, 0), (0, 0), (0, 0), (0, 0)))
    band = WINDOW + Q_BLOCK

    def one_block(qb):
        s0 = qb * Q_BLOCK
        t = s0 + jnp.arange(Q_BLOCK)
        return nsa_core(lax.dynamic_slice_in_dim(q, s0, Q_BLOCK),
                        lax.dynamic_slice_in_dim(gate, s0, Q_BLOCK), t, kc, vc, c_pos, n_blocks, fetch,
                        lax.dynamic_slice_in_dim(kv_w_pad, s0, band), s0 - WINDOW + jnp.arange(band))

    o = lax.map(one_block, jnp.arange(seq // Q_BLOCK))
    return o.reshape(seq, N_HEADS, HEAD_DIM)


def nsa_sample_seq(args, pool_c, pool_s, layer, cmp_w1, cmp_w2, cmp_pos):
    q, gate, kv_c, kv_s, kv_w, win_buf, pages = args
    n_new = q.shape[0]
    n_pages = pages.shape[0]
    past = n_pages * PAGE_SIZE
    past_c = pool_c[layer, pages].reshape((past,) + kv_c.shape[1:]).astype(kv_c.dtype)
    kc, vc, c_pos = compress_blocks(jnp.concatenate([past_c, kv_c], axis=0), cmp_w1, cmp_w2, cmp_pos)
    n_blocks = -(-(past + n_new) // L_SEL)
    g_idx = jnp.arange(N_KV_HEADS)[None, :, None, None]

    def fetch(tok):
        page = pages[jnp.clip(tok // PAGE_SIZE, 0, n_pages - 1)]
        old = pool_s[layer, page, tok % PAGE_SIZE, :, g_idx].astype(kv_s.dtype)
        new = kv_s[jnp.clip(tok - past, 0, n_new - 1), :, g_idx]
        return jnp.where((tok < past)[..., None, None], old, new)

    n_win = win_buf.shape[0]
    kw = jnp.concatenate([win_buf.astype(kv_w.dtype), kv_w], axis=0)
    t = past + jnp.arange(n_new)
    o = nsa_core(q, gate, t, kc, vc, c_pos, n_blocks, fetch, kw, past - n_win + jnp.arange(n_win + n_new))
    return o, kw[n_new:]


def mixer_inputs(h, w_in):
    lead = h.shape[:-1]
    q, kv_c, kv_s, kv_w, g_nsa, xr, yr, g_att, g_rnn = jnp.split(h @ w_in, IN_SPLITS, axis=-1)
    kv_shape = lead + (2, N_KV_HEADS, HEAD_DIM)
    return (q.reshape(lead + (N_HEADS, HEAD_DIM)), jax.nn.sigmoid(g_nsa).reshape(lead + (3, N_HEADS)),
            kv_c.reshape(kv_shape), kv_s.reshape(kv_shape), kv_w.reshape(kv_shape), xr, yr, g_att, g_rnn)


def rglru_branch(xr, yr, conv_buf, h0, conv_w, conv_b, wa, ba, wx, bx, lam):
    n_b, n_s, _ = xr.shape
    x_ext = jnp.concatenate([conv_buf.astype(xr.dtype), xr], axis=1)
    xc = conv_b + x_ext[:, :n_s] * conv_w[0]
    for k in range(1, CONV_W):
        xc = xc + x_ext[:, k:k + n_s] * conv_w[k]
    xb = xc.reshape(n_b, n_s, N_RNN_BLOCKS, RNN_BLOCK)
    r = jax.nn.sigmoid(jnp.einsum('bsnc,ncd->bsnd', xb, wa).reshape(n_b, n_s, D_RNN) + ba)
    i = jax.nn.sigmoid(jnp.einsum('bsnc,ncd->bsnd', xb, wx).reshape(n_b, n_s, D_RNN) + bx)
    log_a = -LRU_C * r.astype(jnp.float32) * jax.nn.softplus(-lam.astype(jnp.float32))
    a = jnp.exp(log_a)
    u = jnp.sqrt(-jnp.expm1(2.0 * log_a)) * (i * xc).astype(jnp.float32)

    def step(h, au):
        h = au[0] * h + au[1]
        return h, h

    h_last, hs = lax.scan(step, h0.astype(jnp.float32), (jnp.swapaxes(a, 0, 1), jnp.swapaxes(u, 0, 1)))
    out = jnp.swapaxes(hs, 0, 1).astype(xr.dtype) * jax.nn.gelu(yr)
    return out, x_ext[:, -(CONV_W - 1):], h_last


def mixer_output(o_attn, rnn_out, g_att, g_rnn, w_attn_out, w_rnn_out, w_o):
    lead = o_attn.shape[:-2]
    att = o_attn.reshape(lead + (Q_DIM,)) @ w_attn_out
    rec = rnn_out @ w_rnn_out
    return (jax.nn.sigmoid(g_att) * att + jax.nn.sigmoid(g_rnn) * rec) @ w_o


def setup_inputs(seed: int = 0) -> dict:
    key = jax.random.key(seed)
    ks = jax.random.split(key, 34)

    def nrm(i, shape, scale):
        return jax.random.normal(ks[i], shape, jnp.float32) * scale

    def gain(i):
        return 1.0 + nrm(i, (DEPTH, D_MODEL), 0.02)

    n_pages = PAST_LEN // PAGE_SIZE
    n_pool = (DEC_BATCH * n_pages * 5 + 3) // 4
    n_win = min(WINDOW, PAST_LEN)
    kv_row = (2, N_KV_HEADS, HEAD_DIM)
    page_table = jax.random.permutation(ks[7], n_pool)[:DEC_BATCH * n_pages].reshape(DEC_BATCH, n_pages).astype(jnp.int32)
    u = jax.random.uniform(ks[25], (DEPTH, D_RNN), jnp.float32, 0.9, 0.999)
    s = u ** (1.0 / LRU_C)
    lam = jnp.log(s) - jnp.log1p(-s)
    return {
        'x_prompt': nrm(0, (BATCH, SEQ, D_MODEL), 1.0),
        'x_sample': nrm(1, (DEC_BATCH, DEC_SEQ, D_MODEL), 1.0),
        'cache_cmp_kv': nrm(2, (DEPTH, n_pool, PAGE_SIZE) + kv_row, 1.0),
        'cache_slc_kv': nrm(3, (DEPTH, n_pool, PAGE_SIZE) + kv_row, 1.0),
        'cache_win_kv': nrm(4, (DEPTH, DEC_BATCH, n_win) + kv_row, 1.0),
        'state_conv': nrm(5, (DEPTH, DEC_BATCH, CONV_W - 1, D_RNN), 1.0),
        'state_h': nrm(6, (DEPTH, DEC_BATCH, D_RNN), 0.5),
        'page_table': page_table,
        'ffn1_norm_pre': gain(8),
        'ffn1_norm_post': gain(9),
        'ffn1_w_gate': nrm(10, (DEPTH, D_MODEL, D_FF), D_MODEL ** -0.5),
        'ffn1_w_up': nrm(11, (DEPTH, D_MODEL, D_FF), D_MODEL ** -0.5),
        'ffn1_w_down': nrm(12, (DEPTH, D_FF, D_MODEL), D_FF ** -0.5),
        'mix_norm_pre': gain(13),
        'mix_norm_post': gain(14),
        'w_in': nrm(15, (DEPTH, D_MODEL, D_IN), D_MODEL ** -0.5),
        'cmp_w1': nrm(16, (DEPTH, 2, L_CMP, HEAD_DIM, HEAD_DIM), (L_CMP * HEAD_DIM) ** -0.5),
        'cmp_w2': nrm(17, (DEPTH, 2, HEAD_DIM, HEAD_DIM), (2.0 / HEAD_DIM) ** 0.5),
        'cmp_pos': nrm(18, (DEPTH, L_CMP, 2, HEAD_DIM), 0.5),
        'conv_w': nrm(19, (DEPTH, CONV_W, D_RNN), CONV_W ** -0.5),
        'conv_b': nrm(20, (DEPTH, D_RNN), 0.01),
        'lru_wa': nrm(21, (DEPTH, N_RNN_BLOCKS, RNN_BLOCK, RNN_BLOCK), RNN_BLOCK ** -0.5),
        'lru_ba': nrm(22, (DEPTH, D_RNN), 0.01),
        'lru_wx': nrm(23, (DEPTH, N_RNN_BLOCKS, RNN_BLOCK, RNN_BLOCK), RNN_BLOCK ** -0.5),
        'lru_bx': nrm(24, (DEPTH, D_RNN), 0.01),
        'lru_lambda': lam,
        'w_attn_out': nrm(26, (DEPTH, Q_DIM, D_MODEL), Q_DIM ** -0.5),
        'w_rnn_out': nrm(27, (DEPTH, D_RNN, D_MODEL), D_RNN ** -0.5),
        'w_o': nrm(28, (DEPTH, D_MODEL, D_MODEL), D_MODEL ** -0.5),
        'ffn2_norm_pre': gain(29),
        'ffn2_norm_post': gain(30),
        'ffn2_w_gate': nrm(31, (DEPTH, D_MODEL, D_FF), D_MODEL ** -0.5),
        'ffn2_w_up': nrm(32, (DEPTH, D_MODEL, D_FF), D_MODEL ** -0.5),
        'ffn2_w_down': nrm(33, (DEPTH, D_FF, D_MODEL), D_FF ** -0.5),
    }


def reference(x_prompt, x_sample, cache_cmp_kv, cache_slc_kv, cache_win_kv, state_conv, state_h, page_table,
              ffn1_norm_pre, ffn1_norm_post, ffn1_w_gate, ffn1_w_up, ffn1_w_down,
              mix_norm_pre, mix_norm_post, w_in, cmp_w1, cmp_w2, cmp_pos, conv_w, conv_b,
              lru_wa, lru_ba, lru_wx, lru_bx, lru_lambda, w_attn_out, w_rnn_out, w_o,
              ffn2_norm_pre, ffn2_norm_post, ffn2_w_gate, ffn2_w_up, ffn2_w_down):
    yp, ys = x_prompt, x_sample
    l_p_cmp, l_p_slc, l_p_win, l_p_conv, l_p_h = [], [], [], [], []
    l_s_cmp, l_s_slc, l_s_win, l_s_conv, l_s_h = [], [], [], [], []
    for l in range(DEPTH):
        yp = half_ffn(yp, ffn1_norm_pre[l], ffn1_norm_post[l], ffn1_w_gate[l], ffn1_w_up[l], ffn1_w_down[l])
        ys = half_ffn(ys, ffn1_norm_pre[l], ffn1_norm_post[l], ffn1_w_gate[l], ffn1_w_up[l], ffn1_w_down[l])
        lru_args = (conv_w[l], conv_b[l], lru_wa[l], lru_ba[l], lru_wx[l], lru_bx[l], lru_lambda[l])
        q, gate, kv_c, kv_s, kv_w, xr, yr, g_att, g_rnn = mixer_inputs(rmsnorm(yp, mix_norm_pre[l]), w_in[l])
        o = lax.map(functools.partial(nsa_prompt_seq, cmp_w1=cmp_w1[l], cmp_w2=cmp_w2[l], cmp_pos=cmp_pos[l]),
                    (q, gate, kv_c, kv_s, kv_w))
        rnn, conv_new, h_new = rglru_branch(xr, yr, jnp.zeros((xr.shape[0], CONV_W - 1, D_RNN), xr.dtype),
                                            jnp.zeros((xr.shape[0], D_RNN), jnp.float32), *lru_args)
        yp = yp + rmsnorm(mixer_output(o, rnn, g_att, g_rnn, w_attn_out[l], w_rnn_out[l], w_o[l]), mix_norm_post[l])
        l_p_cmp.append(kv_c)
        l_p_slc.append(kv_s)
        l_p_win.append(kv_w[:, -min(WINDOW, kv_w.shape[1]):])
        l_p_conv.append(conv_new)
        l_p_h.append(h_new)
        q, gate, kv_c, kv_s, kv_w, xr, yr, g_att, g_rnn = mixer_inputs(rmsnorm(ys, mix_norm_pre[l]), w_in[l])
        o, win_new = lax.map(functools.partial(nsa_sample_seq, pool_c=cache_cmp_kv, pool_s=cache_slc_kv, layer=l,
                                               cmp_w1=cmp_w1[l], cmp_w2=cmp_w2[l], cmp_pos=cmp_pos[l]),
                             (q, gate, kv_c, kv_s, kv_w, cache_win_kv[l], page_table))
        rnn, conv_new, h_new = rglru_branch(xr, yr, state_conv[l], state_h[l], *lru_args)
        ys = ys + rmsnorm(mixer_output(o, rnn, g_att, g_rnn, w_attn_out[l], w_rnn_out[l], w_o[l]), mix_norm_post[l])
        l_s_cmp.append(kv_c)
        l_s_slc.append(kv_s)
        l_s_win.append(win_new)
        l_s_conv.append(conv_new)
        l_s_h.append(h_new)
        yp = half_ffn(yp, ffn2_norm_pre[l], ffn2_norm_post[l], ffn2_w_gate[l], ffn2_w_up[l], ffn2_w_down[l])
        ys = half_ffn(ys, ffn2_norm_pre[l], ffn2_norm_post[l], ffn2_w_gate[l], ffn2_w_up[l], ffn2_w_down[l])
    y_prompt, y_sample = yp, ys
    prompt_cmp_kv = jnp.stack(l_p_cmp)
    prompt_slc_kv = jnp.stack(l_p_slc)
    prompt_win_kv = jnp.stack(l_p_win)
    prompt_conv = jnp.stack(l_p_conv)
    prompt_h = jnp.stack(l_p_h)
    sample_cmp_kv = jnp.stack(l_s_cmp)
    sample_slc_kv = jnp.stack(l_s_slc)
    sample_win_kv = jnp.stack(l_s_win)
    sample_conv = jnp.stack(l_s_conv)
    sample_h = jnp.stack(l_s_h)
    return (y_prompt, y_sample, prompt_cmp_kv, prompt_slc_kv, prompt_win_kv, prompt_conv, prompt_h,
            sample_cmp_kv, sample_slc_kv, sample_win_kv, sample_conv, sample_h)
```

```python
import functools

import jax
import jax.numpy as jnp
from jax import lax
from jax.experimental import pallas as pl
from jax.experimental.pallas import tpu as pltpu

F32 = jnp.float32
BF16 = jnp.bfloat16
I32 = jnp.int32

N_HEADS = 16
N_KV = 4
DH = 64
HPG = N_HEADS // N_KV
L_CMP = 32
STRIDE = 16
L_SEL = 64
N_SEL = 16
WINDOW = 512
Q_BLK = 128
PAGE = 128
CONV_W = 4
LRU_C = 8.0
EPS = 1e-6
NEG = -1e30
MASK_BIG = 1e30
M_INIT = -1e25
SCALE = DH ** -0.5
LANE = 128
POS_COLS = 6
VMEM_LIMIT = 48 * 1024 * 1024


def _cparams(n_axes):
    return pltpu.CompilerParams(dimension_semantics=("arbitrary",) * n_axes, vmem_limit_bytes=VMEM_LIMIT)


def _dot(a, b):
    return jnp.dot(a, b, preferred_element_type=F32)


def _dot_nt(a, b):
    return lax.dot_general(a, b, (((1,), (1,)), ((), ())), preferred_element_type=F32)


def _mm(a, w):
    if w.dtype == BF16:
        return _dot(a.astype(BF16), w)
    return jnp.dot(a, w, preferred_element_type=F32, precision=lax.Precision.HIGHEST)


def _rms(x, g):
    ms = jnp.mean(x * x, axis=-1, keepdims=True)
    return x * lax.rsqrt(ms + EPS) * g


def _full(shape):
    return pl.BlockSpec(shape, lambda *_: (0,) * len(shape))


def _ffn_kernel(x_ref, gpre_ref, gpost_ref, wg_ref, wu_ref, wd_ref, o_ref, xn_ref, acc_ref):
    f = pl.program_id(1)

    @pl.when(f == 0)
    def _():
        xn_ref[...] = _rms(x_ref[...], gpre_ref[...]).astype(xn_ref.dtype)
        acc_ref[...] = jnp.zeros_like(acc_ref)

    xn = xn_ref[...]
    gate = _mm(xn, wg_ref[...])
    up = _mm(xn, wu_ref[...])
    hid = (gate * jax.nn.sigmoid(gate)) * up
    acc_ref[...] += _mm(hid, wd_ref[...])

    @pl.when(f == pl.num_programs(1) - 1)
    def _():
        o_ref[...] = x_ref[...] + 0.5 * _rms(acc_ref[...], gpost_ref[...])


def _half_ffn(x, g_pre, g_post, w_gate, w_up, w_down, *, tm, tf):
    n, d = x.shape
    dff = w_gate.shape[1]
    xn_dtype = BF16 if w_gate.dtype == BF16 else F32
    return pl.pallas_call(
        _ffn_kernel,
        grid=(n // tm, dff // tf),
        in_specs=[
            pl.BlockSpec((tm, d), lambda i, f: (i, 0)),
            pl.BlockSpec((1, d), lambda i, f: (0, 0)),
            pl.BlockSpec((1, d), lambda i, f: (0, 0)),
            pl.BlockSpec((d, tf), lambda i, f: (0, f)),
            pl.BlockSpec((d, tf), lambda i, f: (0, f)),
            pl.BlockSpec((tf, d), lambda i, f: (f, 0)),
        ],
        out_specs=pl.BlockSpec((tm, d), lambda i, f: (i, 0)),
        out_shape=jax.ShapeDtypeStruct((n, d), F32),
        scratch_shapes=[pltpu.VMEM((tm, d), xn_dtype), pltpu.VMEM((tm, d), F32)],
        compiler_params=_cparams(2),
        name="half_ffn",
    )(x, g_pre.reshape(1, d), g_post.reshape(1, d), w_gate, w_up, w_down)


def _norm_proj_kernel(x_ref, g_ref, w_ref, o_ref):
    o_ref[...] = _mm(_rms(x_ref[...], g_ref[...]), w_ref[...])


def _norm_proj(x, g, w, *, tn):
    n, d = x.shape
    cols = w.shape[1]
    return pl.pallas_call(
        _norm_proj_kernel,
        grid=(cols // tn,),
        in_specs=[_full((n, d)), _full((1, d)), pl.BlockSpec((d, tn), lambda j: (0, j))],
        out_specs=pl.BlockSpec((n, tn), lambda j: (0, j)),
        out_shape=jax.ShapeDtypeStruct((n, cols), F32),
        compiler_params=_cparams(1),
        name="norm_proj",
    )(x, g.reshape(1, d), w)


def _inproj_attn_kernel(x_ref, g_ref, wq_ref, qb_ref, wkc_ref, wks_ref, wksa_ref, wkw_ref, wkwa_ref, wg_ref,
                        pos_s_ref, pos_w_ref,
                        q_ref, kvc_ref, kvcb_ref, kvs_ref, ksa_ref, vst_ref, kvw_ref, kwa_ref, vwt_ref, gt_ref):
    xn = _rms(x_ref[0], g_ref[...]).astype(BF16)
    tm = xn.shape[0]
    q_ref[0] = (_dot(xn, wq_ref[...]) + qb_ref[...]).astype(BF16)
    kvc = _dot(xn, wkc_ref[...])
    kvc_ref[0] = kvc
    kvcb_ref[0] = kvc.astype(BF16)
    kvs = _dot(xn, wks_ref[...])
    kvs_ref[0] = kvs
    kvw = _dot(xn, wkw_ref[...])
    kvw_ref[0] = kvw
    ka = ksa_ref.shape[-1]
    for g in range(N_KV):
        ksa_ref[0, g] = (_dot(xn, wksa_ref[:, g * ka:(g + 1) * ka]) + pos_s_ref[...]).astype(BF16)
        kwa_ref[0, g] = (_dot(xn, wkwa_ref[:, g * LANE:(g + 1) * LANE]) + pos_w_ref[...]).astype(BF16)
    kvd = N_KV * DH
    vst = kvs[:, kvd:].T.astype(BF16)
    tk = vst_ref.shape[-1]
    for j in range(tm // tk):
        vst_ref[0, :, j] = vst[:, j * tk:(j + 1) * tk].reshape(N_KV, DH, tk)
    vwt = kvw[:, kvd:].T.astype(BF16)
    for j in range(tm // LANE):
        vwt_ref[0, :, j] = vwt[:, j * LANE:(j + 1) * LANE].reshape(N_KV, DH, LANE)
    gt_ref[0] = jax.nn.sigmoid(_dot(xn, wg_ref[...])).T


def _inproj_attn(yp, g_pre, wts, pos_s, pos_w, *, tm, tk):
    b, s, d = yp.shape
    ka = pos_s.shape[1]
    kv = 2 * N_KV * DH
    qa = N_HEADS * LANE
    row = lambda i, j: (i, j, 0)
    outs = [
        (jax.ShapeDtypeStruct((b, s, qa), BF16), pl.BlockSpec((1, tm, qa), row)),
        (jax.ShapeDtypeStruct((b, s, kv), F32), pl.BlockSpec((1, tm, kv), row)),
        (jax.ShapeDtypeStruct((b, s, kv), BF16), pl.BlockSpec((1, tm, kv), row)),
        (jax.ShapeDtypeStruct((b, s, kv), F32), pl.BlockSpec((1, tm, kv), row)),
        (jax.ShapeDtypeStruct((b, N_KV, s, ka), BF16), pl.BlockSpec((1, N_KV, tm, ka), lambda i, j: (i, 0, j, 0))),
        (jax.ShapeDtypeStruct((b, N_KV, s // tk, DH, tk), BF16),
         pl.BlockSpec((1, N_KV, tm // tk, DH, tk), lambda i, j: (i, 0, j, 0, 0))),
        (jax.ShapeDtypeStruct((b, s, kv), F32), pl.BlockSpec((1, tm, kv), row)),
        (jax.ShapeDtypeStruct((b, N_KV, s, LANE), BF16), pl.BlockSpec((1, N_KV, tm, LANE), lambda i, j: (i, 0, j, 0))),
        (jax.ShapeDtypeStruct((b, N_KV, s // LANE, DH, LANE), BF16),
         pl.BlockSpec((1, N_KV, tm // LANE, DH, LANE), lambda i, j: (i, 0, j, 0, 0))),
        (jax.ShapeDtypeStruct((b, LANE, s), F32), pl.BlockSpec((1, LANE, tm), lambda i, j: (i, 0, j))),
    ]
    w_specs = [_full(w.shape) for w in wts]
    return pl.pallas_call(
        _inproj_attn_kernel,
        grid=(b, s // tm),
        in_specs=[pl.BlockSpec((1, tm, d), row), _full((1, d))] + w_specs
        + [pl.BlockSpec((tm, ka), lambda i, j: (j, 0)), pl.BlockSpec((tm, LANE), lambda i, j: (j, 0))],
        out_specs=[o[1] for o in outs],
        out_shape=[o[0] for o in outs],
        compiler_params=_cparams(2),
        name="inproj_attn",
    )(yp, g_pre.reshape(1, d), *wts, pos_s, pos_w)


def _inproj_rnn_kernel(x_ref, g_ref, w_ref, xr_ref, yr_ref, ga_ref, gr_ref):
    xn = _rms(x_ref[...], g_ref[...]).astype(BF16)
    d = xr_ref.shape[-1]
    for i, o in enumerate((xr_ref, yr_ref, ga_ref, gr_ref)):
        o[...] = _dot(xn, w_ref[:, i * d:(i + 1) * d])


def _inproj_rnn(x, g_pre, w, *, tm):
    n, d = x.shape
    spec = pl.BlockSpec((tm, d), lambda i: (i, 0))
    return pl.pallas_call(
        _inproj_rnn_kernel,
        grid=(n // tm,),
        in_specs=[spec, _full((1, d)), _full(w.shape)],
        out_specs=[spec] * 4,
        out_shape=[jax.ShapeDtypeStruct((n, d), F32)] * 4,
        compiler_params=_cparams(1),
        name="inproj_rnn",
    )(x, g_pre.reshape(1, d), w)


def _compress_tail(acc_a, acc_b, bias, w2_ref, w2a_ref, posc_ref, n_c, kca_ref, vct_ref):
    ncp = acc_a[0].shape[0]
    valid = lax.broadcasted_iota(I32, (ncp, 1), 0) < n_c
    hid = []
    for s in range(2):
        pre = acc_a[s] + pltpu.roll(acc_b[s], ncp - 1, 0) + bias[s]
        hid.append(jax.nn.gelu(pre).astype(BF16))
    for g in range(N_KV):
        kc = _dot(hid[0], w2a_ref[g]) + posc_ref[...]
        kca_ref[0, g] = jnp.where(valid, kc, 0.0).astype(BF16)
    vc = jnp.where(valid, _dot(hid[1], w2_ref[...]), 0.0)
    vct_ref[0] = vc.T.astype(BF16).reshape(N_KV, DH, ncp)


def _compress_sums(get, w1_ref, s, half):
    acc = None
    for l in range(STRIDE):
        d = _dot(get(l), w1_ref[s, half, l])
        acc = d if acc is None else acc + d
    return acc


def _compress_prompt_kernel(x_ref, pea_ref, peb_ref, w1_ref, w2_ref, w2a_ref, posc_ref, kca_ref, vct_ref, *, n_c):
    gw = N_KV * DH
    row = 2 * gw
    acc_a, acc_b, bias = [], [], []
    for s in range(2):
        sl = lambda l: slice(l * row + s * gw, l * row + (s + 1) * gw)
        acc_a.append(_compress_sums(lambda l: x_ref[0, :, sl(l)], w1_ref, s, 0))
        acc_b.append(_compress_sums(lambda l: x_ref[0, :, sl(l)], w1_ref, s, 1))
        pb = (_compress_sums(lambda l: pea_ref[:, sl(l)], w1_ref, s, 0)
              + _compress_sums(lambda l: peb_ref[:, sl(l)], w1_ref, s, 1))
        bias.append(pb[0:1])
    _compress_tail(acc_a, acc_b, bias, w2_ref, w2a_ref, posc_ref, n_c, kca_ref, vct_ref)


def _compress_prompt(kvc_chunks, pe_a, pe_b, w1bd, w2v, w2aug, posc, *, n_c):
    b, ncp, width = kvc_chunks.shape
    return pl.pallas_call(
        functools.partial(_compress_prompt_kernel, n_c=n_c),
        grid=(b,),
        in_specs=[pl.BlockSpec((1, ncp, width), lambda i: (i, 0, 0)), _full(pe_a.shape), _full(pe_b.shape),
                  _full(w1bd.shape), _full(w2v.shape), _full(w2aug.shape), _full(posc.shape)],
        out_specs=[pl.BlockSpec((1, N_KV, ncp, LANE), lambda i: (i, 0, 0, 0)),
                   pl.BlockSpec((1, N_KV, DH, ncp), lambda i: (i, 0, 0, 0))],
        out_shape=[jax.ShapeDtypeStruct((b, N_KV, ncp, LANE), BF16),
                   jax.ShapeDtypeStruct((b, N_KV, DH, ncp), BF16)],
        compiler_params=_cparams(1),
        name="compress_prompt",
    )(kvc_chunks, pe_a, pe_b, w1bd, w2v, w2aug, posc)


def _stack_heads(qa):
    return jnp.concatenate([qa[:, LANE * i:LANE * (i + 1)] for i in range(HPG)], axis=0)


def _lane_heads(g8):
    return jnp.concatenate([g8[i:i + 1, :] for i in range(HPG)], axis=1)


def _unstack_heads(x):
    return jnp.concatenate([x[:, LANE * i:LANE * (i + 1)] for i in range(HPG)], axis=0)


def _nsa_select_kernel(q_ref, kc_ref, vc_ref, g0_ref, ov_ref, oc_ref, mb_ref, fl_ref, *, nb, tk):
    qb = pl.program_id(2)
    t0 = qb * Q_BLK
    qs = _stack_heads(q_ref[0])
    s = _dot_nt(kc_ref[0, 0], qs)
    c_pos = lax.broadcasted_iota(I32, s.shape, 0) * STRIDE + (L_CMP - 1)
    t = t0 + (lax.broadcasted_iota(I32, s.shape, 1) & (Q_BLK - 1))
    ok = c_pos <= t
    s = jnp.where(ok, s, NEG)
    m = jnp.max(s, axis=0, keepdims=True)
    e = jnp.where(ok, jnp.exp(s - m), 0.0)
    l = jnp.sum(e, axis=0, keepdims=True)
    p = e * (1.0 / jnp.where(l > 0.0, l, 1.0))
    oc = _dot(vc_ref[0, 0], p.astype(BF16)) * _lane_heads(g0_ref[0])
    oc_ref[0] = _unstack_heads(oc)
    psum = p[:, 0:LANE] + p[:, LANE:2 * LANE] + p[:, 2 * LANE:3 * LANE] + p[:, 3 * LANE:4 * LANE]
    hi = psum.astype(BF16)
    lo = (psum - hi.astype(F32)).astype(BF16)
    imp = _dot(ov_ref[...], hi) + _dot(ov_ref[...], lo)
    j = lax.broadcasted_iota(I32, imp.shape, 0).astype(F32)
    cur = ((t0 + lax.broadcasted_iota(I32, imp.shape, 1)) // L_SEL).astype(F32)
    forced = (j == 0.0) | (j == cur) | (j == cur - 1.0)
    score = jnp.where(j <= cur, jnp.where(forced, jnp.inf, imp), -jnp.inf)
    sel_mask = jnp.zeros(imp.shape, F32)
    for _ in range(min(N_SEL, nb)):
        mx = jnp.max(score, axis=0, keepdims=True)
        idx = jnp.min(jnp.where(score == mx, j, float(nb)), axis=0, keepdims=True)
        hit = j == idx
        sel_mask = jnp.where(hit, 1.0, sel_mask)
        score = jnp.where(hit, -jnp.inf, score)
    sel_mask = jnp.where(j <= cur, sel_mask, 0.0)
    mb_ref[0, 0] = ((sel_mask - 1.0) * MASK_BIG).T.astype(BF16)
    per = tk // L_SEL
    ra = jnp.broadcast_to(jnp.max(sel_mask, axis=1, keepdims=True), sel_mask.shape)
    r = ra
    for sft in range(1, per):
        r = jnp.maximum(r, pltpu.roll(ra, nb - sft, 0))
    lane = lax.broadcasted_iota(I32, imp.shape, 1)
    fl = jnp.sum(jnp.where(j == (per * lane).astype(F32), r, 0.0), axis=0, keepdims=True)
    fl_ref[0, 0, 0] = jnp.broadcast_to(fl, (8, LANE))


def _nsa_select(q_aug, kca, vct, gt, ov, *, nb, tk):
    b, s, _ = q_aug.shape
    ncp = kca.shape[2]
    nqb = s // Q_BLK
    return pl.pallas_call(
        functools.partial(_nsa_select_kernel, nb=nb, tk=tk),
        grid=(b, N_KV, nqb),
        in_specs=[
            pl.BlockSpec((1, Q_BLK, HPG * LANE), lambda i, g, q: (i, q, g)),
            pl.BlockSpec((1, 1, ncp, LANE), lambda i, g, q: (i, g, 0, 0)),
            pl.BlockSpec((1, 1, DH, ncp), lambda i, g, q: (i, g, 0, 0)),
            pl.BlockSpec((1, 8, Q_BLK), lambda i, g, q: (i, g, q)),
            _full(ov.shape),
        ],
        out_specs=[
            pl.BlockSpec((1, HPG * DH, Q_BLK), lambda i, g, q: (i, g, q)),
            pl.BlockSpec((1, 1, Q_BLK, nb), lambda i, g, q: (i, g, q, 0)),
            pl.BlockSpec((1, 1, 1, 8, LANE), lambda i, g, q: (i, g, q, 0, 0)),
        ],
        out_shape=[
            jax.ShapeDtypeStruct((b, N_HEADS * DH, s), F32),
            jax.ShapeDtypeStruct((b, N_KV, s, nb), BF16),
            jax.ShapeDtypeStruct((b, N_KV, nqb, 8, LANE), F32),
        ],
        compiler_params=_cparams(3),
        name="nsa_select",
    )(q_aug, kca, vct, gt, ov)


def _nsa_attend_kernel(bits_ref, q_ref, mb_ref, ks_ref, vs_ref, kw_ref, vw_ref, g1_ref, g2_ref, oc_ref, o_ref,
                       m_s, l_s, acc_s, *, tk, n_win):
    bi, g, qb = pl.program_id(0), pl.program_id(1), pl.program_id(2)
    nqb = pl.num_programs(2)
    t0 = qb * Q_BLK
    qs = _stack_heads(q_ref[0])
    mb = mb_ref[0, 0]
    qfull = jnp.concatenate([qs, jnp.concatenate([mb] * HPG, axis=0)], axis=1)
    t_lane = t0 + (lax.broadcasted_iota(I32, (1, HPG * Q_BLK), 1) & (Q_BLK - 1))
    m_s[...] = jnp.full(m_s.shape, M_INIT, F32)
    l_s[...] = jnp.zeros_like(l_s)
    acc_s[...] = jnp.zeros_like(acc_s)
    bits = bits_ref[(bi * N_KV + g) * nqb + qb]
    n_tiles = (t0 + Q_BLK + tk - 1) // tk

    def body(kt, carry):
        @pl.when(((bits >> kt) & 1) == 1)
        def _():
            k = ks_ref[0, 0, pl.ds(pl.multiple_of(kt * tk, tk), tk), :]
            s = _dot_nt(k, qfull)
            key = kt * tk + lax.broadcasted_iota(I32, s.shape, 0)
            s = jnp.where(key <= t_lane, s, NEG)
            m_old = m_s[...]
            m_new = jnp.maximum(m_old, jnp.max(s, axis=0, keepdims=True))
            alpha = jnp.exp(m_old - m_new)
            p = jnp.exp(s - m_new)
            l_s[...] = alpha * l_s[...] + jnp.sum(p, axis=0, keepdims=True)
            acc_s[...] = alpha * acc_s[...] + _dot(vs_ref[0, 0, kt], p.astype(BF16))
            m_s[...] = m_new
        return carry

    lax.fori_loop(0, n_tiles, body, 0)
    o_sel = acc_s[...] * (1.0 / l_s[...])
    start = jnp.maximum(t0 - WINDOW, 0)
    kwin = kw_ref[0, 0, pl.ds(pl.multiple_of(start, LANE), n_win * LANE), :]
    sw = _dot_nt(kwin, qs)
    dist = t_lane - (start + lax.broadcasted_iota(I32, sw.shape, 0))
    sw = jnp.where((dist >= 0) & (dist < WINDOW), sw, NEG)
    mw = jnp.max(sw, axis=0, keepdims=True)
    pw = jnp.exp(sw - mw)
    lw = jnp.sum(pw, axis=0, keepdims=True)
    pwb = pw.astype(BF16)
    tile0 = start // LANE
    accw = None
    for i in range(n_win):
        d = _dot(vw_ref[0, 0, tile0 + i], pwb[i * LANE:(i + 1) * LANE])
        accw = d if accw is None else accw + d
    o_win = accw * (1.0 / lw)
    o_t = o_sel * _lane_heads(g1_ref[0]) + o_win * _lane_heads(g2_ref[0])
    o_all = _unstack_heads(o_t) + oc_ref[0]
    o_ref[0] = o_all.T.astype(BF16)


def _nsa_attend(bits, q_aug, mb, ksa, vst, kwa, vwt, gt, oc_t, *, tk):
    b, s, _ = q_aug.shape
    nb = mb.shape[-1]
    ka = ksa.shape[-1]
    nqb = s // Q_BLK
    n_win = WINDOW // LANE + 1
    grid_spec = pltpu.PrefetchScalarGridSpec(
        num_scalar_prefetch=1,
        grid=(b, N_KV, nqb),
        in_specs=[
            pl.BlockSpec((1, Q_BLK, HPG * LANE), lambda i, g, q, *_: (i, q, g)),
            pl.BlockSpec((1, 1, Q_BLK, nb), lambda i, g, q, *_: (i, g, q, 0)),
            pl.BlockSpec((1, 1, s, ka), lambda i, g, q, *_: (i, g, 0, 0)),
            pl.BlockSpec((1, 1, s // tk, DH, tk), lambda i, g, q, *_: (i, g, 0, 0, 0)),
            pl.BlockSpec((1, 1, s, LANE), lambda i, g, q, *_: (i, g, 0, 0)),
            pl.BlockSpec((1, 1, s // LANE, DH, LANE), lambda i, g, q, *_: (i, g, 0, 0, 0)),
            pl.BlockSpec((1, 8, Q_BLK), lambda i, g, q, *_: (i, N_KV + g, q)),
            pl.BlockSpec((1, 8, Q_BLK), lambda i, g, q, *_: (i, 2 * N_KV + g, q)),
            pl.BlockSpec((1, HPG * DH, Q_BLK), lambda i, g, q, *_: (i, g, q)),
        ],
        out_specs=pl.BlockSpec((1, Q_BLK, HPG * DH), lambda i, g, q, *_: (i, q, g)),
        scratch_shapes=[pltpu.VMEM((1, HPG * Q_BLK), F32), pltpu.VMEM((1, HPG * Q_BLK), F32),
                        pltpu.VMEM((DH, HPG * Q_BLK), F32)],
    )
    return pl.pallas_call(
        functools.partial(_nsa_attend_kernel, tk=tk, n_win=n_win),
        grid_spec=grid_spec,
        out_shape=jax.ShapeDtypeStruct((b, s, N_HEADS * DH), BF16),
        compiler_params=_cparams(3),
        name="nsa_attend",
    )(bits, q_aug, mb, ksa, vst, kwa, vwt, gt, gt, oc_t)


def _lru_gates(xc, wa_ref, wx_ref, ba_ref, bx_ref, lam_ref):
    nt = wa_ref.shape[0]
    wdt = wa_ref.shape[1]
    pre_r, pre_i = [], []
    for i in range(nt):
        xs = xc[:, i * wdt:(i + 1) * wdt]
        pre_r.append(_mm(xs, wa_ref[i]))
        pre_i.append(_mm(xs, wx_ref[i]))
    r = jax.nn.sigmoid(jnp.concatenate(pre_r, axis=1) + ba_ref[...])
    ig = jax.nn.sigmoid(jnp.concatenate(pre_i, axis=1) + bx_ref[...])
    neg_lam = -lam_ref[...]
    softplus = jnp.maximum(neg_lam, 0.0) + jnp.log1p(jnp.exp(-jnp.abs(neg_lam)))
    log_a = -LRU_C * r * softplus
    a = jnp.exp(log_a)
    u = jnp.sqrt(1.0 - jnp.exp(2.0 * log_a)) * (ig * xc)
    return a, u


def _rglru_seq_kernel(xr_ref, yr_ref, cw_ref, cb_ref, wa_ref, wx_ref, ba_ref, bx_ref, lam_ref,
                      o_ref, hl_ref, xbuf, a_s, u_s, h_s):
    ti = pl.program_id(1)
    tt = xr_ref.shape[1]

    @pl.when(ti == 0)
    def _():
        xbuf[0:8, :] = jnp.zeros((8, xbuf.shape[1]), F32)
        h_s[...] = jnp.zeros_like(h_s)

    x = xr_ref[0]
    xbuf[8:8 + tt, :] = x
    xc = cb_ref[...] + xbuf[5:5 + tt, :] * cw_ref[0:1, :]
    xc = xc + xbuf[6:6 + tt, :] * cw_ref[1:2, :]
    xc = xc + xbuf[7:7 + tt, :] * cw_ref[2:3, :]
    xc = xc + x * cw_ref[3:4, :]
    xbuf[0:8, :] = x[tt - 8:tt, :]
    a, u = _lru_gates(xc, wa_ref, wx_ref, ba_ref, bx_ref, lam_ref)
    sub = lax.broadcasted_iota(I32, a.shape, 0) & 7
    for dlt in (1, 2, 4):
        keep = sub >= dlt
        a_sh = jnp.where(keep, pltpu.roll(a, dlt, 0), 1.0)
        u_sh = jnp.where(keep, pltpu.roll(u, dlt, 0), 0.0)
        u = a * u_sh + u
        a = a * a_sh
    a_s[...] = a
    u_s[...] = u

    def grp(i, h):
        r0 = pl.multiple_of(i * 8, 8)
        hr = a_s[pl.ds(r0, 8), :] * h + u_s[pl.ds(r0, 8), :]
        u_s[pl.ds(r0, 8), :] = hr
        return jnp.broadcast_to(hr[7:8, :], hr.shape)

    h = lax.fori_loop(0, tt // 8, grp, h_s[...])
    h_s[...] = h
    o_ref[0] = (u_s[...] * jax.nn.gelu(yr_ref[0])).astype(o_ref.dtype)

    @pl.when(ti == pl.num_programs(1) - 1)
    def _():
        hl_ref[0] = h


def _rglru_seq(xr, yr, conv_w, conv_b, wa_t, wx_t, ba, bx, lam, *, tt):
    b, s, d = xr.shape
    row = lambda i, j: (i, j, 0)
    small = [conv_w, conv_b.reshape(1, d), wa_t, wx_t, ba.reshape(1, d), bx.reshape(1, d), lam.reshape(1, d)]
    return pl.pallas_call(
        _rglru_seq_kernel,
        grid=(b, s // tt),
        in_specs=[pl.BlockSpec((1, tt, d), row), pl.BlockSpec((1, tt, d), row)] + [_full(w.shape) for w in small],
        out_specs=[pl.BlockSpec((1, tt, d), row), pl.BlockSpec((1, 8, d), lambda i, j: (i, 0, 0))],
        out_shape=[jax.ShapeDtypeStruct((b, s, d), BF16), jax.ShapeDtypeStruct((b, 8, d), F32)],
        scratch_shapes=[pltpu.VMEM((tt + 8, d), F32), pltpu.VMEM((tt, d), F32), pltpu.VMEM((tt, d), F32),
                        pltpu.VMEM((8, d), F32)],
        compiler_params=_cparams(2),
        name="rglru_seq",
    )(xr, yr, *small)


def _mixer_out_kernel(x_ref, o_ref, r_ref, ga_ref, gr_ref, wa_ref, wr_ref, wo_ref, gp_ref, y_ref):
    att = _mm(o_ref[...], wa_ref[...])
    rec = _mm(r_ref[...], wr_ref[...])
    mix = jax.nn.sigmoid(ga_ref[...]) * att + jax.nn.sigmoid(gr_ref[...]) * rec
    y_ref[...] = x_ref[...] + _rms(_mm(mix, wo_ref[...]), gp_ref[...])


def _mixer_out(x, o, r, ga, gr, w_att, w_rnn, w_o, g_post, *, tm):
    n, d = x.shape
    spec = pl.BlockSpec((tm, d), lambda i: (i, 0))
    return pl.pallas_call(
        _mixer_out_kernel,
        grid=(n // tm,),
        in_specs=[spec] * 5 + [_full(w_att.shape), _full(w_rnn.shape), _full(w_o.shape), _full((1, d))],
        out_specs=spec,
        out_shape=jax.ShapeDtypeStruct((n, d), F32),
        compiler_params=_cparams(1),
        name="mixer_out",
    )(x, o, r, ga, gr, w_att, w_rnn, w_o, g_post.reshape(1, d))


def _sample_cmp_kernel(pt_ref, *refs, n_pg, n_c, t_pos):
    pages = refs[:n_pg]
    (qbd_ref, pea_ref, peb_ref, w1_ref, w2k_ref, w2v_ref, gsum_ref, ovt_ref, slope_ref,
     oc_ref, imp_ref, xs, ya, zb) = refs[n_pg:]
    st = pl.program_id(1)
    rows = n_pg * (PAGE // STRIDE)
    for k in range(0, n_pg, 2):
        pair = jnp.concatenate([pages[k][0], pages[k + 1][0]], axis=0)
        xs[k * 8:(k + 2) * 8, :] = pair.astype(BF16)
    gw = N_KV * DH
    row = 2 * gw
    r0 = pl.multiple_of(st * rows, rows)
    for s in range(2):
        sl = lambda l: slice(l * row + s * gw, l * row + (s + 1) * gw)
        ya[s, pl.ds(r0, rows), :] = _compress_sums(lambda l: xs[:, sl(l)], w1_ref, s, 0)
        zb[s, pl.ds(r0, rows), :] = _compress_sums(lambda l: xs[:, sl(l)], w1_ref, s, 1)

    @pl.when(st == pl.num_programs(1) - 1)
    def _():
        ncp = ya.shape[1]
        hid = []
        for s in range(2):
            sl = lambda l: slice(l * row + s * gw, l * row + (s + 1) * gw)
            bias = (_compress_sums(lambda l: pea_ref[:, sl(l)], w1_ref, s, 0)
                    + _compress_sums(lambda l: peb_ref[:, sl(l)], w1_ref, s, 1))[0:1]
            pre = ya[s] + pltpu.roll(zb[s], ncp - 1, 0) + bias
            hid.append(jax.nn.gelu(pre).astype(BF16))
        kc = _dot(hid[0], w2k_ref[...]).astype(BF16)
        vc = _dot(hid[1], w2v_ref[...]).astype(BF16)
        c_idx = lax.broadcasted_iota(I32, (1, ncp), 1)
        c_pos = c_idx * STRIDE + (L_CMP - 1)
        ok = (c_idx < n_c) & (c_pos <= t_pos)
        s_c = _dot_nt(qbd_ref[0], kc) + slope_ref[:, 0:1] * c_pos.astype(F32)
        s_c = jnp.where(ok, s_c, NEG)
        m = jnp.max(s_c, axis=1, keepdims=True)
        e = jnp.where(ok, jnp.exp(s_c - m), 0.0)
        l = jnp.sum(e, axis=1, keepdims=True)
        p = e * (1.0 / jnp.where(l > 0.0, l, 1.0))
        oc_ref[0] = _dot(p.astype(BF16), vc)
        p_hi = p.astype(BF16)
        p_lo = (p - p_hi.astype(F32)).astype(BF16)
        psum = _dot(gsum_ref[...], p_hi) + _dot(gsum_ref[...], p_lo)
        s_hi = psum.astype(BF16)
        s_lo = (psum - s_hi.astype(F32)).astype(BF16)
        imp_ref[0] = _dot(s_hi, ovt_ref[...]) + _dot(s_lo, ovt_ref[...])


def _sample_cmp(page_table, pool, qbd, consts, *, layer, n_pool, n_c, t_pos, n_pg):
    db, n_pages = page_table.shape
    steps = n_pages // n_pg
    ncp = n_pages * (PAGE // STRIDE)
    width = pool.shape[-1]
    nbp = consts[-2].shape[1]

    def page_spec(k):
        return pl.BlockSpec((1, PAGE // STRIDE, width),
                            lambda i, st, pt: (layer * n_pool + pt[i, st * n_pg + k], 0, 0))

    grid_spec = pltpu.PrefetchScalarGridSpec(
        num_scalar_prefetch=1,
        grid=(db, steps),
        in_specs=[page_spec(k) for k in range(n_pg)]
        + [pl.BlockSpec((1, N_HEADS, N_KV * DH), lambda i, st, pt: (i, 0, 0))]
        + [pl.BlockSpec(c.shape, functools.partial(lambda nd, *_: (0,) * nd, c.ndim)) for c in consts],
        out_specs=[pl.BlockSpec((1, N_HEADS, N_KV * DH), lambda i, st, pt: (i, 0, 0)),
                   pl.BlockSpec((1, 8, nbp), lambda i, st, pt: (i, 0, 0))],
        scratch_shapes=[pltpu.VMEM((n_pg * (PAGE // STRIDE), width), BF16),
                        pltpu.VMEM((2, ncp, N_KV * DH), F32), pltpu.VMEM((2, ncp, N_KV * DH), F32)],
    )
    return pl.pallas_call(
        functools.partial(_sample_cmp_kernel, n_pg=n_pg, n_c=n_c, t_pos=t_pos),
        grid_spec=grid_spec,
        out_shape=[jax.ShapeDtypeStruct((db, N_HEADS, N_KV * DH), F32), jax.ShapeDtypeStruct((db, 8, nbp), F32)],
        compiler_params=pltpu.CompilerParams(dimension_semantics=("arbitrary", "arbitrary"),
                                             vmem_limit_bytes=56 * 1024 * 1024),
        name="sample_cmp",
    )(page_table, *([pool] * n_pg), qbd, *consts)


def _sample_topk_kernel(imp_ref, mb_ref, *, nbs, cur):
    imp = imp_ref[...]
    j = lax.broadcasted_iota(I32, imp.shape, 1).astype(F32)
    curf = float(cur)
    forced = (j == 0.0) | (j == curf) | (j == curf - 1.0)
    valid = (j <= curf) & (j < float(nbs))
    score = jnp.where(valid, jnp.where(forced, jnp.inf, imp), -jnp.inf)
    sel = jnp.zeros(imp.shape, F32)
    for _ in range(min(N_SEL, nbs)):
        mx = jnp.max(score, axis=1, keepdims=True)
        idx = jnp.min(jnp.where(score == mx, j, float(imp.shape[1])), axis=1, keepdims=True)
        hit = j == idx
        sel = jnp.where(hit, 1.0, sel)
        score = jnp.where(hit, -jnp.inf, score)
    sel = jnp.where(valid, sel, 0.0)
    mb_ref[...] = (sel - 1.0) * MASK_BIG


def _sample_topk(imp, *, nbs, cur):
    return pl.pallas_call(
        functools.partial(_sample_topk_kernel, nbs=nbs, cur=cur),
        grid=(1,),
        in_specs=[_full(imp.shape)],
        out_specs=_full(imp.shape),
        out_shape=jax.ShapeDtypeStruct(imp.shape, F32),
        compiler_params=_cparams(1),
        name="sample_topk",
    )(imp)


def _sample_attend_kernel(pt_ref, *refs, n_pg, past):
    pages = refs[:n_pg]
    (qbd_ref, mbh_ref, mbn_ref, ex_ref, ksn_ref, win_ref, kwn_ref, gate_ref, oc_ref, slope_ref,
     o_ref, wnew_ref, m_s, l_s, acc_s) = refs[n_pg:]
    st = pl.program_id(1)
    kvd = N_KV * DH
    qbd = qbd_ref[0]
    slope = slope_ref[:, 0:1]

    @pl.when(st == 0)
    def _():
        m_s[...] = jnp.full(m_s.shape, M_INIT, F32)
        l_s[...] = jnp.zeros_like(l_s)
        acc_s[...] = jnp.zeros_like(acc_s)

    kall = jnp.concatenate([pg[0][:, 0:kvd] for pg in pages], axis=0).astype(BF16)
    vall = jnp.concatenate([pg[0][:, kvd:2 * kvd] for pg in pages], axis=0).astype(BF16)
    nkey = n_pg * PAGE
    tok = st * nkey + lax.broadcasted_iota(I32, (1, nkey), 1)
    s = _dot_nt(qbd, kall) + slope * tok.astype(F32) + _dot(mbh_ref[0, 0], ex_ref[...])
    s = jnp.where(tok <= past, s, NEG)
    m_old = m_s[...]
    m_new = jnp.maximum(m_old, jnp.max(s, axis=1, keepdims=True))
    alpha = jnp.exp(m_old - m_new)
    p = jnp.exp(s - m_new)
    l_s[...] = alpha * l_s[...] + jnp.sum(p, axis=1, keepdims=True)
    acc_s[...] = alpha * acc_s[...] + _dot(p.astype(BF16), vall)
    m_s[...] = m_new

    @pl.when(st == pl.num_programs(1) - 1)
    def _():
        qf = qbd.astype(F32)
        t_bias = slope * float(past)
        ksn = ksn_ref[0]
        s_n = jnp.sum(qf * ksn[:, 0:kvd], axis=1, keepdims=True) + t_bias + mbn_ref[0][:, 0:1]
        m_old = m_s[...]
        m_new = jnp.maximum(m_old, s_n)
        alpha = jnp.exp(m_old - m_new)
        p_n = jnp.exp(s_n - m_new)
        l_fin = alpha * l_s[...] + p_n
        o_sel = (alpha * acc_s[...] + p_n * ksn[:, kvd:2 * kvd]) * (1.0 / l_fin)
        win = win_ref[0]
        n_win = win.shape[0]
        kwn = kwn_ref[0]
        r = lax.broadcasted_iota(I32, (1, n_win), 1)
        w_pos = past - n_win + r
        dist = past - w_pos
        s_w = _dot_nt(qbd, win[:, 0:kvd].astype(BF16)) + slope * w_pos.astype(F32)
        s_w = jnp.where((dist < WINDOW) & (w_pos >= 0), s_w, NEG)
        s_wn = jnp.sum(qf * kwn[:, 0:kvd], axis=1, keepdims=True) + t_bias
        m_w = jnp.maximum(jnp.max(s_w, axis=1, keepdims=True), s_wn)
        p_w = jnp.exp(s_w - m_w)
        p_wn = jnp.exp(s_wn - m_w)
        l_w = jnp.sum(p_w, axis=1, keepdims=True) + p_wn
        o_win = (_dot(p_w.astype(BF16), win[:, kvd:2 * kvd].astype(BF16)) + p_wn * kwn[:, kvd:2 * kvd]) * (1.0 / l_w)
        gts = jax.nn.sigmoid(gate_ref[0])
        o_wide = gts[:, 0:1] * oc_ref[0] + gts[:, 1:2] * o_sel + gts[:, 2:3] * o_win
        grp = lax.broadcasted_iota(I32, (N_HEADS, DH), 0) // HPG
        out = jnp.zeros((N_HEADS, DH), F32)
        for g in range(N_KV):
            out = out + jnp.where(grp == g, o_wide[:, g * DH:(g + 1) * DH], 0.0)
        o_ref[0] = out
        rows = lax.broadcasted_iota(I32, win.shape, 0)
        wnew_ref[0] = jnp.where(rows == n_win - 1, jnp.broadcast_to(kwn, win.shape), pltpu.roll(win, n_win - 1, 0))


def _sample_attend(page_table, pool, qbd, mbh, mbn, ex, ksn, win, kwn, gates, oc, slope, *, layer, n_pool, past, n_pg):
    db, n_pages = page_table.shape
    steps = n_pages // n_pg
    kv = pool.shape[-1]
    n_win = win.shape[1]
    per_b = lambda shape: pl.BlockSpec((1,) + shape, lambda i, st, pt: (i,) + (0,) * len(shape))

    def page_spec(k):
        return pl.BlockSpec((1, PAGE, kv), lambda i, st, pt: (layer * n_pool + pt[i, st * n_pg + k], 0, 0))

    grid_spec = pltpu.PrefetchScalarGridSpec(
        num_scalar_prefetch=1,
        grid=(db, steps),
        in_specs=[page_spec(k) for k in range(n_pg)] + [
            per_b((N_HEADS, N_KV * DH)),
            pl.BlockSpec((1, 1, N_HEADS, LANE), lambda i, st, pt: (i, st, 0, 0)),
            per_b((N_HEADS, LANE)),
            pl.BlockSpec(ex.shape, lambda i, st, pt: (0, 0)),
            per_b((1, kv)), per_b((n_win, kv)), per_b((1, kv)), per_b((N_HEADS, LANE)),
            per_b((N_HEADS, N_KV * DH)),
            pl.BlockSpec(slope.shape, lambda i, st, pt: (0, 0)),
        ],
        out_specs=[per_b((N_HEADS, DH)), per_b((n_win, kv))],
        scratch_shapes=[pltpu.VMEM((N_HEADS, 1), F32), pltpu.VMEM((N_HEADS, 1), F32),
                        pltpu.VMEM((N_HEADS, N_KV * DH), F32)],
    )
    return pl.pallas_call(
        functools.partial(_sample_attend_kernel, n_pg=n_pg, past=past),
        grid_spec=grid_spec,
        out_shape=[jax.ShapeDtypeStruct((db, N_HEADS, DH), F32), jax.ShapeDtypeStruct((db, n_win, kv), F32)],
        compiler_params=_cparams(2),
        name="sample_attend",
    )(page_table, *([pool] * n_pg), qbd, mbh, mbn, ex, ksn, win, kwn, gates, oc, slope)


def _rglru_step_kernel(xr_ref, yr_ref, sc_ref, h0_ref, cw_ref, cb_ref, wa_ref, wx_ref, ba_ref, bx_ref, lam_ref,
                       o_ref, h_ref):
    xc = cb_ref[...] + sc_ref[0] * cw_ref[0:1, :]
    xc = xc + sc_ref[1] * cw_ref[1:2, :]
    xc = xc + sc_ref[2] * cw_ref[2:3, :]
    xc = xc + xr_ref[...] * cw_ref[3:4, :]
    a, u = _lru_gates(xc, wa_ref, wx_ref, ba_ref, bx_ref, lam_ref)
    h = a * h0_ref[...] + u
    h_ref[...] = h
    o_ref[...] = h * jax.nn.gelu(yr_ref[...])


def _rglru_step(xr, yr, sc, h0, conv_w, conv_b, wa_t, wx_t, ba, bx, lam):
    n, d = xr.shape
    args = [xr, yr, sc, h0, conv_w, conv_b.reshape(1, d), wa_t, wx_t, ba.reshape(1, d), bx.reshape(1, d),
            lam.reshape(1, d)]
    return pl.pallas_call(
        _rglru_step_kernel,
        grid=(1,),
        in_specs=[_full(a.shape) for a in args],
        out_specs=[_full((n, d)), _full((n, d))],
        out_shape=[jax.ShapeDtypeStruct((n, d), F32)] * 2,
        compiler_params=_cparams(1),
        name="rglru_step",
    )(*args)


def _slopes():
    h = jnp.arange(1, N_HEADS + 1, dtype=F32)
    return jnp.exp2(-8.0 * h / N_HEADS)


def _split3(x):
    p1 = x.astype(BF16).astype(F32)
    p2 = (x - p1).astype(BF16).astype(F32)
    p3 = (x - p1 - p2).astype(BF16).astype(F32)
    return p1, p2, p3


def _pos_cols(pos):
    hi = ((pos // L_SEL) * L_SEL).astype(F32)
    lo = (pos % L_SEL).astype(F32)
    return jnp.stack([hi, lo] * 3, axis=-1)


def _blockdiag(w, n):
    eye = jnp.eye(n, dtype=w.dtype)
    out = jnp.einsum("...ab,gh->...gahb", w, eye)
    return out.reshape(w.shape[:-2] + (n * w.shape[-2], n * w.shape[-1]))


def _lru_tiles(w):
    nb, c, _ = w.shape
    per = 4
    t = w.reshape(nb // per, per, c, c)
    eye = jnp.eye(per, dtype=w.dtype)
    return jnp.einsum("tpab,pq->tpaqb", t, eye).reshape(nb // per, per * c, per * c)


def _split_w_in(w_in):
    d = w_in.shape[0]
    sizes = (N_HEADS * DH, 2 * N_KV * DH, 2 * N_KV * DH, 2 * N_KV * DH, 3 * N_HEADS, d, d, d, d)
    out, o = [], 0
    for sz in sizes:
        out.append(w_in[:, o:o + sz])
        o += sz
    return out


def _prompt_layer(yp, cfg, p):
    b, s, d = yp.shape
    n = b * s
    tk = cfg["tk"]
    nb = s // L_SEL
    ka = LANE + nb
    n_c = (s - L_CMP) // STRIDE + 1
    ncp = s // STRIDE
    yp = _half_ffn(yp.reshape(n, d), p["f1_pre"], p["f1_post"], p["f1_wg"], p["f1_wu"], p["f1_wd"],
                   tm=cfg["ffn_tm"], tf=cfg["ffn_tf"]).reshape(b, s, d)
    (q_aug, kvc, kvc_bf, kvs, ksa, vst, kvw, kwa, vwt, gt) = _inproj_attn(
        yp, p["mix_pre"], p["attn_w"], p["pos_s"], p["pos_w"], tm=cfg["pa_tm"], tk=tk)
    xr, yr, ga, gr = _inproj_rnn(yp.reshape(n, d), p["mix_pre"], p["rnn_w"], tm=cfg["pr_tm"])
    kca, vct = _compress_prompt(kvc_bf.reshape(b, ncp, STRIDE * 2 * N_KV * DH), p["pe_a"], p["pe_b"], p["w1bd"], p["w2v"],
                                p["w2aug"], p["posc"], n_c=n_c)
    oc_t, mb, fl = _nsa_select(q_aug, kca, vct, gt, p["ov"], nb=nb, tk=tk)
    nt = s // tk
    flags = (fl[:, :, :, 0, :nt] > 0.5).astype(I32)
    bits = jnp.sum(flags << jnp.arange(nt, dtype=I32), axis=-1).reshape(-1)
    o = _nsa_attend(bits, q_aug, mb, ksa, vst, kwa, vwt, gt, oc_t, tk=tk)
    rnn, h_last = _rglru_seq(xr.reshape(b, s, d), yr.reshape(b, s, d), p["conv_w"], p["conv_b"], p["wa_t"], p["wx_t"],
                             p["ba"], p["bx"], p["lam"], tt=cfg["lru_tt"])
    yp = _mixer_out(yp.reshape(n, d), o.reshape(n, d), rnn.reshape(n, d), ga, gr, p["w_att"], p["w_rnn"], p["w_o"],
                    p["mix_post"], tm=cfg["mo_tm"])
    yp = _half_ffn(yp, p["f2_pre"], p["f2_post"], p["f2_wg"], p["f2_wu"], p["f2_wd"],
                   tm=cfg["ffn_tm"], tf=cfg["ffn_tf"]).reshape(b, s, d)
    kv_shape = (b, s, 2, N_KV, DH)
    n_keep = min(WINDOW, s)
    state = (kvc.reshape(kv_shape), kvs.reshape(kv_shape), kvw.reshape(kv_shape)[:, s - n_keep:],
             xr.reshape(b, s, d)[:, s - (CONV_W - 1):], h_last[:, 0])
    return yp, state


def _prep_prompt_params(l, s, w):
    d = w["w_in"].shape[1]
    nb = s // L_SEL
    ka = LANE + nb
    ncp = s // STRIDE
    wq, wkc, wks, wkw, wg, wxr, wyr, wga, wgr = _split_w_in(w["w_in"][l])
    zeros = lambda *sh: jnp.zeros(sh, F32)
    wq_aug = jnp.concatenate([(wq * SCALE).reshape(d, N_HEADS, DH), zeros(d, N_HEADS, LANE - DH)], axis=-1)
    wq_aug = wq_aug.reshape(d, N_HEADS * LANE).astype(BF16)
    sl = jnp.stack([c for piece in _split3(_slopes()) for c in (piece, piece)], axis=-1)
    qbias = jnp.concatenate([zeros(N_HEADS, DH), sl, zeros(N_HEADS, LANE - DH - POS_COLS)], axis=-1)
    qbias = qbias.reshape(1, N_HEADS * LANE)
    kpart = lambda wk: wk[:, :N_KV * DH].reshape(d, N_KV, DH)
    wks_aug = jnp.concatenate([kpart(wks), zeros(d, N_KV, ka - DH)], axis=-1).reshape(d, N_KV * ka).astype(BF16)
    wkw_aug = jnp.concatenate([kpart(wkw), zeros(d, N_KV, LANE - DH)], axis=-1).reshape(d, N_KV * LANE).astype(BF16)
    wg4 = wg.reshape(d, 3, N_KV, HPG)
    wg_pad = jnp.concatenate([wg4, zeros(d, 3, N_KV, 8 - HPG)], axis=-1).reshape(d, 3 * N_KV * 8)
    wg_pad = jnp.concatenate([wg_pad, zeros(d, LANE - 3 * N_KV * 8)], axis=-1).astype(BF16)
    tok = jnp.arange(s, dtype=I32)
    pos6 = _pos_cols(tok)
    onehot = (tok[:, None] // L_SEL == jnp.arange(nb, dtype=I32)[None, :]).astype(F32)
    pos_s = jnp.concatenate([zeros(s, DH), pos6, zeros(s, LANE - DH - POS_COLS), onehot], axis=-1)
    pos_w = jnp.concatenate([zeros(s, DH), pos6, zeros(s, LANE - DH - POS_COLS)], axis=-1)
    cpos = jnp.arange(ncp, dtype=I32) * STRIDE + (L_CMP - 1)
    posc = jnp.concatenate([zeros(ncp, DH), _pos_cols(cpos), zeros(ncp, LANE - DH - POS_COLS)], axis=-1)
    w1 = w["cmp_w1"][l]
    w1bd = _blockdiag(w1, N_KV).reshape(2, 2, STRIDE, N_KV * DH, N_KV * DH).astype(BF16)
    w2 = w["cmp_w2"][l]
    w2bd = _blockdiag(w2, N_KV).astype(BF16)
    eye = jnp.eye(N_KV, dtype=F32)
    w2aug = jnp.einsum("de,gh->ghde", w2[0], eye).reshape(N_KV, N_KV * DH, DH)
    w2aug = jnp.concatenate([w2aug, zeros(N_KV, N_KV * DH, LANE - DH)], axis=-1).astype(BF16)
    pe = w["cmp_pos"][l]
    pe_rows = jnp.broadcast_to(pe[:, :, None, :], (L_CMP, 2, N_KV, DH)).reshape(2, STRIDE * 2 * N_KV * DH)
    pe_a = jnp.broadcast_to(pe_rows[0:1], (16, pe_rows.shape[1])).astype(BF16)
    pe_b = jnp.broadcast_to(pe_rows[1:2], (16, pe_rows.shape[1])).astype(BF16)
    c_idx = jnp.arange(ncp, dtype=I32)[None, :] * STRIDE
    j_idx = jnp.arange(nb, dtype=I32)[:, None]
    ov = ((c_idx < (j_idx + 1) * L_SEL) & (c_idx + L_CMP > j_idx * L_SEL)).astype(BF16)
    return {
        "f1_pre": w["ffn1_norm_pre"][l], "f1_post": w["ffn1_norm_post"][l],
        "f1_wg": w["ffn1_w_gate"][l].astype(BF16), "f1_wu": w["ffn1_w_up"][l].astype(BF16),
        "f1_wd": w["ffn1_w_down"][l].astype(BF16),
        "f2_pre": w["ffn2_norm_pre"][l], "f2_post": w["ffn2_norm_post"][l],
        "f2_wg": w["ffn2_w_gate"][l].astype(BF16), "f2_wu": w["ffn2_w_up"][l].astype(BF16),
        "f2_wd": w["ffn2_w_down"][l].astype(BF16),
        "mix_pre": w["mix_norm_pre"][l], "mix_post": w["mix_norm_post"][l],
        "attn_w": [wq_aug, qbias, wkc.astype(BF16), wks.astype(BF16), wks_aug, wkw.astype(BF16), wkw_aug, wg_pad],
        "rnn_w": jnp.concatenate([wxr, wyr, wga, wgr], axis=1).astype(BF16),
        "pos_s": pos_s, "pos_w": pos_w, "posc": posc, "pe_a": pe_a, "pe_b": pe_b, "w1bd": w1bd, "w2v": w2bd[1], "w2aug": w2aug,
        "ov": ov,
        "conv_w": w["conv_w"][l], "conv_b": w["conv_b"][l],
        "wa_t": _lru_tiles(w["lru_wa"][l]).astype(BF16), "wx_t": _lru_tiles(w["lru_wx"][l]).astype(BF16),
        "ba": w["lru_ba"][l], "bx": w["lru_bx"][l], "lam": w["lru_lambda"][l],
        "w_att": w["w_attn_out"][l].astype(BF16), "w_rnn": w["w_rnn_out"][l].astype(BF16),
        "w_o": w["w_o"][l].astype(BF16),
    }


def _sample_layer(ys, l, w, cache_cmp, cache_slc, win_buf, state_conv, state_h, page_table):
    db, d = ys.shape
    n_pages = page_table.shape[1]
    past = n_pages * PAGE
    n_pool = cache_cmp.shape[1]
    n_c = (past + 1 - L_CMP) // STRIDE + 1
    assert (n_c - 1) * STRIDE + L_CMP <= past, "compressed blocks must lie inside the paged history"
    assert past % 1024 == 0 and past >= WINDOW and past // L_SEL + 1 <= 256
    n_win = win_buf.shape[1]
    nbs = -(-(past + 1) // L_SEL)
    cur = past // L_SEL
    nbp = 256
    ncp = past // STRIDE
    ys = _half_ffn(ys, w["ffn1_norm_pre"][l], w["ffn1_norm_post"][l], w["ffn1_w_gate"][l], w["ffn1_w_up"][l],
                   w["ffn1_w_down"][l], tm=db, tf=256)
    w_in = w["w_in"][l]
    d_in = w_in.shape[1]
    pad = (-d_in) % 256
    proj = _norm_proj(ys, w["mix_norm_pre"][l], jnp.pad(w_in, ((0, 0), (0, pad))), tn=256)
    sizes = (N_HEADS * DH, 2 * N_KV * DH, 2 * N_KV * DH, 2 * N_KV * DH, 3 * N_HEADS, d, d, d, d)
    parts, o = [], 0
    for sz in sizes:
        parts.append(proj[:, o:o + sz])
        o += sz
    q, kvc, kvs, kvw, g_nsa, xr, yr, ga, gr = parts
    onehot = (jnp.arange(N_HEADS)[:, None] // HPG == jnp.arange(N_KV)[None, :]).astype(F32)
    qbd = ((q * SCALE).reshape(db, N_HEADS, 1, DH) * onehot[None, :, :, None]).reshape(db, N_HEADS, N_KV * DH)
    qbd = qbd.astype(BF16)
    slope = jnp.broadcast_to(_slopes()[:, None], (N_HEADS, LANE))
    w1 = w["cmp_w1"][l]
    w1bd = _blockdiag(w1, N_KV).reshape(2, 2, STRIDE, N_KV * DH, N_KV * DH).astype(BF16)
    w2bd = _blockdiag(w["cmp_w2"][l], N_KV).astype(BF16)
    pe = w["cmp_pos"][l]
    pe_rows = jnp.broadcast_to(pe[:, :, None, :], (L_CMP, 2, N_KV, DH)).reshape(2, STRIDE * 2 * N_KV * DH)
    pe_a = jnp.broadcast_to(pe_rows[0:1], (16, pe_rows.shape[1])).astype(BF16)
    pe_b = jnp.broadcast_to(pe_rows[1:2], (16, pe_rows.shape[1])).astype(BF16)
    gsum = (jnp.arange(8)[:, None] == jnp.arange(N_HEADS)[None, :] // HPG).astype(BF16)
    c_idx = jnp.arange(ncp, dtype=I32)[:, None] * STRIDE
    j_idx = jnp.arange(nbp, dtype=I32)[None, :]
    ovt = ((c_idx < (j_idx + 1) * L_SEL) & (c_idx + L_CMP > j_idx * L_SEL) & (j_idx < nbs)).astype(BF16)
    pool_c = cache_cmp.reshape(cache_cmp.shape[0] * n_pool, PAGE // STRIDE, STRIDE * 2 * N_KV * DH)
    oc, imp = _sample_cmp(page_table, pool_c, qbd, [pe_a, pe_b, w1bd, w2bd[0], w2bd[1], gsum, ovt, slope],
                          layer=l, n_pool=n_pool, n_c=n_c, t_pos=past, n_pg=min(32, n_pages))
    mb = _sample_topk(imp.reshape(db * 8, nbp), nbs=nbs, cur=cur).reshape(db, 8, nbp)[:, :N_KV]
    n_pg = 8
    steps = n_pages // n_pg
    per_step = n_pg * PAGE // L_SEL
    mb_h = jnp.repeat(mb, HPG, axis=1)
    mbh = mb_h[:, :, :steps * per_step].reshape(db, N_HEADS, steps, per_step).transpose(0, 2, 1, 3)
    mbh = jnp.pad(mbh, ((0, 0), (0, 0), (0, 0), (0, LANE - per_step))).astype(BF16)
    mbn = jnp.pad(mb_h[:, :, cur:cur + 1], ((0, 0), (0, 0), (0, LANE - 1)))
    ex = (jnp.arange(LANE, dtype=I32)[:, None] == jnp.arange(n_pg * PAGE, dtype=I32)[None, :] // L_SEL).astype(BF16)
    gates = jnp.pad(g_nsa.reshape(db, 3, N_HEADS).transpose(0, 2, 1), ((0, 0), (0, 0), (0, LANE - 3)))
    pool_s = cache_slc.reshape(cache_slc.shape[0] * n_pool, PAGE, 2 * N_KV * DH)
    o_att, win_new = _sample_attend(page_table, pool_s, qbd, mbh, mbn, ex, kvs.reshape(db, 1, -1),
                                    win_buf.reshape(db, n_win, -1), kvw.reshape(db, 1, -1), gates, oc, slope,
                                    layer=l, n_pool=n_pool, past=past, n_pg=n_pg)
    rnn, h_new = _rglru_step(xr, yr, state_conv.transpose(1, 0, 2), state_h, w["conv_w"][l], w["conv_b"][l],
                             _lru_tiles(w["lru_wa"][l]), _lru_tiles(w["lru_wx"][l]), w["lru_ba"][l], w["lru_bx"][l],
                             w["lru_lambda"][l])
    ys = _mixer_out(ys, o_att.reshape(db, N_HEADS * DH), rnn, ga, gr, w["w_attn_out"][l], w["w_rnn_out"][l],
                    w["w_o"][l], w["mix_norm_post"][l], tm=db)
    ys = _half_ffn(ys, w["ffn2_norm_pre"][l], w["ffn2_norm_post"][l], w["ffn2_w_gate"][l], w["ffn2_w_up"][l],
                   w["ffn2_w_down"][l], tm=db, tf=256)
    kv_shape = (db, 1, 2, N_KV, DH)
    conv_new = jnp.concatenate([state_conv[:, 1:], xr[:, None, :]], axis=1)
    state = (kvc.reshape(kv_shape), kvs.reshape(kv_shape), win_new.reshape(db, n_win, 2, N_KV, DH), conv_new, h_new)
    return ys, state


def _prompt_cfg(s):
    return {"tk": 256, "ffn_tm": 512, "ffn_tf": 1408, "pa_tm": 256, "pr_tm": 512, "lru_tt": 512, "mo_tm": 512}


def kernel(x_prompt, x_sample, cache_cmp_kv, cache_slc_kv, cache_win_kv, state_conv, state_h, page_table,
           ffn1_norm_pre, ffn1_norm_post, ffn1_w_gate, ffn1_w_up, ffn1_w_down,
           mix_norm_pre, mix_norm_post, w_in, cmp_w1, cmp_w2, cmp_pos, conv_w, conv_b,
           lru_wa, lru_ba, lru_wx, lru_bx, lru_lambda, w_attn_out, w_rnn_out, w_o,
           ffn2_norm_pre, ffn2_norm_post, ffn2_w_gate, ffn2_w_up, ffn2_w_down):
    w = dict(ffn1_norm_pre=ffn1_norm_pre, ffn1_norm_post=ffn1_norm_post, ffn1_w_gate=ffn1_w_gate,
             ffn1_w_up=ffn1_w_up, ffn1_w_down=ffn1_w_down, mix_norm_pre=mix_norm_pre, mix_norm_post=mix_norm_post,
             w_in=w_in, cmp_w1=cmp_w1, cmp_w2=cmp_w2, cmp_pos=cmp_pos, conv_w=conv_w, conv_b=conv_b,
             lru_wa=lru_wa, lru_ba=lru_ba, lru_wx=lru_wx, lru_bx=lru_bx, lru_lambda=lru_lambda,
             w_attn_out=w_attn_out, w_rnn_out=w_rnn_out, w_o=w_o, ffn2_norm_pre=ffn2_norm_pre,
             ffn2_norm_post=ffn2_norm_post, ffn2_w_gate=ffn2_w_gate, ffn2_w_up=ffn2_w_up, ffn2_w_down=ffn2_w_down)
    depth = w_in.shape[0]
    b, s, d = x_prompt.shape
    assert s % 512 == 0 and s >= WINDOW + Q_BLK and s // L_SEL + LANE <= 256
    yp = x_prompt
    p_states = []
    for l in range(depth):
        yp, st = _prompt_layer(yp, _prompt_cfg(s), _prep_prompt_params(l, s, w))
        p_states.append(st)
    p_out = [jnp.stack([st[i] for st in p_states]) for i in range(5)]
    db = x_sample.shape[0]
    assert x_sample.shape[1] == 1, "one new token per running sequence"
    ys = x_sample.reshape(db, d)
    s_states = []
    for l in range(depth):
        ys, st = _sample_layer(ys, l, w, cache_cmp_kv, cache_slc_kv, cache_win_kv[l], state_conv[l], state_h[l],
                               page_table)
        s_states.append(st)
    s_out = [jnp.stack([st[i] for st in s_states]) for i in range(5)]
    return (yp, ys.reshape(db, 1, d), *p_out, *s_out)
```

```python
import functools

import jax
import jax.numpy as jnp
from jax import lax
from jax.experimental import pallas as pl
from jax.experimental.pallas import tpu as pltpu

F32 = jnp.float32
BF16 = jnp.bfloat16
I32 = jnp.int32

N_HEADS = 16
N_KV = 4
DH = 64
HPG = N_HEADS // N_KV
L_CMP = 32
STRIDE = 16
L_SEL = 64
N_SEL = 16
WINDOW = 512
Q_BLK = 128
PAGE = 128
CONV_W = 4
LRU_C = 8.0
EPS = 1e-6
NEG = -1e30
MASK_BIG = 1e30
M_INIT = -1e25
SCALE = DH ** -0.5
LANE = 128
POS_COLS = 6
VMEM_LIMIT = 48 * 1024 * 1024


def _cparams(n_axes):
    return pltpu.CompilerParams(dimension_semantics=("arbitrary",) * n_axes, vmem_limit_bytes=VMEM_LIMIT)


def _dot(a, b):
    return jnp.dot(a, b, preferred_element_type=F32)


def _dot_nt(a, b):
    return lax.dot_general(a, b, (((1,), (1,)), ((), ())), preferred_element_type=F32)


def _mm(a, w):
    if w.dtype == BF16:
        return _dot(a.astype(BF16), w)
    return jnp.dot(a, w, preferred_element_type=F32, precision=lax.Precision.HIGHEST)


def _rms(x, g):
    ms = jnp.mean(x * x, axis=-1, keepdims=True)
    return x * lax.rsqrt(ms + EPS) * g


def _full(shape):
    return pl.BlockSpec(shape, lambda *_: (0,) * len(shape))


def _ffn_kernel(x_ref, gpre_ref, gpost_ref, wg_ref, wu_ref, wd_ref, o_ref, xn_ref, acc_ref):
    f = pl.program_id(1)

    @pl.when(f == 0)
    def _():
        xn_ref[...] = _rms(x_ref[...], gpre_ref[...]).astype(xn_ref.dtype)
        acc_ref[...] = jnp.zeros_like(acc_ref)

    xn = xn_ref[...]
    gate = _mm(xn, wg_ref[...])
    up = _mm(xn, wu_ref[...])
    hid = (gate * jax.nn.sigmoid(gate)) * up
    acc_ref[...] += _mm(hid, wd_ref[...])

    @pl.when(f == pl.num_programs(1) - 1)
    def _():
        o_ref[...] = x_ref[...] + 0.5 * _rms(acc_ref[...], gpost_ref[...])


def _half_ffn(x, g_pre, g_post, w_gate, w_up, w_down, *, tm, tf):
    n, d = x.shape
    dff = w_gate.shape[1]
    xn_dtype = BF16 if w_gate.dtype == BF16 else F32
    return pl.pallas_call(
        _ffn_kernel,
        grid=(n // tm, dff // tf),
        in_specs=[
            pl.BlockSpec((tm, d), lambda i, f: (i, 0)),
            pl.BlockSpec((1, d), lambda i, f: (0, 0)),
            pl.BlockSpec((1, d), lambda i, f: (0, 0)),
            pl.BlockSpec((d, tf), lambda i, f: (0, f)),
            pl.BlockSpec((d, tf), lambda i, f: (0, f)),
            pl.BlockSpec((tf, d), lambda i, f: (f, 0)),
        ],
        out_specs=pl.BlockSpec((tm, d), lambda i, f: (i, 0)),
        out_shape=jax.ShapeDtypeStruct((n, d), F32),
        scratch_shapes=[pltpu.VMEM((tm, d), xn_dtype), pltpu.VMEM((tm, d), F32)],
        compiler_params=_cparams(2),
        name="half_ffn",
    )(x, g_pre.reshape(1, d), g_post.reshape(1, d), w_gate, w_up, w_down)


def _norm_proj_kernel(x_ref, g_ref, w_ref, o_ref):
    o_ref[...] = _mm(_rms(x_ref[...], g_ref[...]), w_ref[...])


def _norm_proj(x, g, w, *, tn):
    n, d = x.shape
    cols = w.shape[1]
    return pl.pallas_call(
        _norm_proj_kernel,
        grid=(cols // tn,),
        in_specs=[_full((n, d)), _full((1, d)), pl.BlockSpec((d, tn), lambda j: (0, j))],
        out_specs=pl.BlockSpec((n, tn), lambda j: (0, j)),
        out_shape=jax.ShapeDtypeStruct((n, cols), F32),
        compiler_params=_cparams(1),
        name="norm_proj",
    )(x, g.reshape(1, d), w)


def _inproj_attn_kernel(x_ref, g_ref, wq_ref, qb_ref, wkc_ref, wks_ref, wksa_ref, wkw_ref, wkwa_ref, wg_ref,
                        pos_s_ref, pos_w_ref,
                        q_ref, kvc_ref, kvcb_ref, kvs_ref, ksa_ref, vst_ref, kvw_ref, kwa_ref, vwt_ref, gt_ref):
    xn = _rms(x_ref[0], g_ref[...]).astype(BF16)
    tm = xn.shape[0]
    q_ref[0] = (_dot(xn, wq_ref[...]) + qb_ref[...]).astype(BF16)
    kvc = _dot(xn, wkc_ref[...])
    kvc_ref[0] = kvc
    kvcb_ref[0] = kvc.astype(BF16)
    kvs = _dot(xn, wks_ref[...])
    kvs_ref[0] = kvs
    kvw = _dot(xn, wkw_ref[...])
    kvw_ref[0] = kvw
    ka = ksa_ref.shape[-1]
    for g in range(N_KV):
        ksa_ref[0, g] = (_dot(xn, wksa_ref[:, g * ka:(g + 1) * ka]) + pos_s_ref[...]).astype(BF16)
        kwa_ref[0, g] = (_dot(xn, wkwa_ref[:, g * LANE:(g + 1) * LANE]) + pos_w_ref[...]).astype(BF16)
    kvd = N_KV * DH
    vst = kvs[:, kvd:].T.astype(BF16)
    tk = vst_ref.shape[-1]
    for j in range(tm // tk):
        vst_ref[0, :, j] = vst[:, j * tk:(j + 1) * tk].reshape(N_KV, DH, tk)
    vwt = kvw[:, kvd:].T.astype(BF16)
    for j in range(tm // LANE):
        vwt_ref[0, :, j] = vwt[:, j * LANE:(j + 1) * LANE].reshape(N_KV, DH, LANE)
    gt_ref[0] = jax.nn.sigmoid(_dot(xn, wg_ref[...])).T


def _inproj_attn(yp, g_pre, wts, pos_s, pos_w, *, tm, tk):
    b, s, d = yp.shape
    ka = pos_s.shape[1]
    kv = 2 * N_KV * DH
    qa = N_HEADS * LANE
    row = lambda i, j: (i, j, 0)
    outs = [
        (jax.ShapeDtypeStruct((b, s, qa), BF16), pl.BlockSpec((1, tm, qa), row)),
        (jax.ShapeDtypeStruct((b, s, kv), F32), pl.BlockSpec((1, tm, kv), row)),
        (jax.ShapeDtypeStruct((b, s, kv), BF16), pl.BlockSpec((1, tm, kv), row)),
        (jax.ShapeDtypeStruct((b, s, kv), F32), pl.BlockSpec((1, tm, kv), row)),
        (jax.ShapeDtypeStruct((b, N_KV, s, ka), BF16), pl.BlockSpec((1, N_KV, tm, ka), lambda i, j: (i, 0, j, 0))),
        (jax.ShapeDtypeStruct((b, N_KV, s // tk, DH, tk), BF16),
         pl.BlockSpec((1, N_KV, tm // tk, DH, tk), lambda i, j: (i, 0, j, 0, 0))),
        (jax.ShapeDtypeStruct((b, s, kv), F32), pl.BlockSpec((1, tm, kv), row)),
        (jax.ShapeDtypeStruct((b, N_KV, s, LANE), BF16), pl.BlockSpec((1, N_KV, tm, LANE), lambda i, j: (i, 0, j, 0))),
        (jax.ShapeDtypeStruct((b, N_KV, s // LANE, DH, LANE), BF16),
         pl.BlockSpec((1, N_KV, tm // LANE, DH, LANE), lambda i, j: (i, 0, j, 0, 0))),
        (jax.ShapeDtypeStruct((b, LANE, s), F32), pl.BlockSpec((1, LANE, tm), lambda i, j: (i, 0, j))),
    ]
    w_specs = [_full(w.shape) for w in wts]
    return pl.pallas_call(
        _inproj_attn_kernel,
        grid=(b, s // tm),
        in_specs=[pl.BlockSpec((1, tm, d), row), _full((1, d))] + w_specs
        + [pl.BlockSpec((tm, ka), lambda i, j: (j, 0)), pl.BlockSpec((tm, LANE), lambda i, j: (j, 0))],
        out_specs=[o[1] for o in outs],
        out_shape=[o[0] for o in outs],
        compiler_params=_cparams(2),
        name="inproj_attn",
    )(yp, g_pre.reshape(1, d), *wts, pos_s, pos_w)


def _inproj_rnn_kernel(x_ref, g_ref, w_ref, xr_ref, yr_ref, ga_ref, gr_ref):
    xn = _rms(x_ref[...], g_ref[...]).astype(BF16)
    d = xr_ref.shape[-1]
    for i, o in enumerate((xr_ref, yr_ref, ga_ref, gr_ref)):
        o[...] = _dot(xn, w_ref[:, i * d:(i + 1) * d])


def _inproj_rnn(x, g_pre, w, *, tm):
    n, d = x.shape
    spec = pl.BlockSpec((tm, d), lambda i: (i, 0))
    return pl.pallas_call(
        _inproj_rnn_kernel,
        grid=(n // tm,),
        in_specs=[spec, _full((1, d)), _full(w.shape)],
        out_specs=[spec] * 4,
        out_shape=[jax.ShapeDtypeStruct((n, d), F32)] * 4,
        compiler_params=_cparams(1),
        name="inproj_rnn",
    )(x, g_pre.reshape(1, d), w)


def _compress_tail(acc_a, acc_b, bias, w2_ref, w2a_ref, posc_ref, n_c, kca_ref, vct_ref):
    ncp = acc_a[0].shape[0]
    valid = lax.broadcasted_iota(I32, (ncp, 1), 0) < n_c
    hid = []
    for s in range(2):
        pre = acc_a[s] + pltpu.roll(acc_b[s], ncp - 1, 0) + bias[s]
        hid.append(jax.nn.gelu(pre).astype(BF16))
    for g in range(N_KV):
        kc = _dot(hid[0], w2a_ref[g]) + posc_ref[...]
        kca_ref[0, g] = jnp.where(valid, kc, 0.0).astype(BF16)
    vc = jnp.where(valid, _dot(hid[1], w2_ref[...]), 0.0)
    vct_ref[0] = vc.T.astype(BF16).reshape(N_KV, DH, ncp)


def _compress_sums(get, w1_ref, s, half):
    acc = None
    for l in range(STRIDE):
        d = _dot(get(l), w1_ref[s, half, l])
        acc = d if acc is None else acc + d
    return acc


def _compress_prompt_kernel(x_ref, pea_ref, peb_ref, w1_ref, w2_ref, w2a_ref, posc_ref, kca_ref, vct_ref, *, n_c):
    gw = N_KV * DH
    row = 2 * gw
    acc_a, acc_b, bias = [], [], []
    for s in range(2):
        sl = lambda l: slice(l * row + s * gw, l * row + (s + 1) * gw)
        acc_a.append(_compress_sums(lambda l: x_ref[0, :, sl(l)], w1_ref, s, 0))
        acc_b.append(_compress_sums(lambda l: x_ref[0, :, sl(l)], w1_ref, s, 1))
        pb = (_compress_sums(lambda l: pea_ref[:, sl(l)], w1_ref, s, 0)
              + _compress_sums(lambda l: peb_ref[:, sl(l)], w1_ref, s, 1))
        bias.append(pb[0:1])
    _compress_tail(acc_a, acc_b, bias, w2_ref, w2a_ref, posc_ref, n_c, kca_ref, vct_ref)


def _compress_prompt(kvc_chunks, pe_a, pe_b, w1bd, w2v, w2aug, posc, *, n_c):
    b, ncp, width = kvc_chunks.shape
    return pl.pallas_call(
        functools.partial(_compress_prompt_kernel, n_c=n_c),
        grid=(b,),
        in_specs=[pl.BlockSpec((1, ncp, width), lambda i: (i, 0, 0)), _full(pe_a.shape), _full(pe_b.shape),
                  _full(w1bd.shape), _full(w2v.shape), _full(w2aug.shape), _full(posc.shape)],
        out_specs=[pl.BlockSpec((1, N_KV, ncp, LANE), lambda i: (i, 0, 0, 0)),
                   pl.BlockSpec((1, N_KV, DH, ncp), lambda i: (i, 0, 0, 0))],
        out_shape=[jax.ShapeDtypeStruct((b, N_KV, ncp, LANE), BF16),
                   jax.ShapeDtypeStruct((b, N_KV, DH, ncp), BF16)],
        compiler_params=_cparams(1),
        name="compress_prompt",
    )(kvc_chunks, pe_a, pe_b, w1bd, w2v, w2aug, posc)


def _stack_heads(qa):
    return jnp.concatenate([qa[:, LANE * i:LANE * (i + 1)] for i in range(HPG)], axis=0)


def _lane_heads(g8):
    return jnp.concatenate([g8[i:i + 1, :] for i in range(HPG)], axis=1)


def _unstack_heads(x):
    return jnp.concatenate([x[:, LANE * i:LANE * (i + 1)] for i in range(HPG)], axis=0)


def _nsa_select_kernel(q_ref, kc_ref, vc_ref, g0_ref, ov_ref, oc_ref, mb_ref, fl_ref, *, nb, tk, n_var):
    qb = pl.program_id(1)
    nqb = pl.num_programs(1)
    ncp = kc_ref.shape[2]
    for v in range(n_var):
        lo_q = (v * nqb) // n_var
        hi_q = ((v + 1) * nqb) // n_var

        @pl.when((qb >= lo_q) & (qb < hi_q))
        def _():
            ncols = ((v + 1) * ncp) // n_var
            for g in range(N_KV):
                _nsa_select_group(g, qb, ncols, q_ref, kc_ref, vc_ref, g0_ref, ov_ref, oc_ref, mb_ref, fl_ref,
                                  nb=nb, tk=tk)


def _nsa_select_group(g, qb, ncols, q_ref, kc_ref, vc_ref, g0_ref, ov_ref, oc_ref, mb_ref, fl_ref, *, nb, tk):
    t0 = qb * Q_BLK
    gq = HPG * LANE
    qs = _stack_heads(q_ref[0, :, g * gq:(g + 1) * gq])
    s = _dot_nt(kc_ref[0, g, 0:ncols, :], qs)
    c_pos = lax.broadcasted_iota(I32, s.shape, 0) * STRIDE + (L_CMP - 1)
    t = t0 + (lax.broadcasted_iota(I32, s.shape, 1) & (Q_BLK - 1))
    ok = c_pos <= t
    s = jnp.where(ok, s, NEG)
    m = jnp.max(s, axis=0, keepdims=True)
    e = jnp.where(ok, jnp.exp(s - m), 0.0)
    l = jnp.sum(e, axis=0, keepdims=True)
    p = e * (1.0 / jnp.where(l > 0.0, l, 1.0))
    oc = _dot(vc_ref[0, g, :, 0:ncols], p.astype(BF16)) * _lane_heads(g0_ref[0, g * 8:(g + 1) * 8, :])
    oc_ref[0, g * HPG * DH:(g + 1) * HPG * DH, :] = _unstack_heads(oc)
    psum = p[:, 0:LANE] + p[:, LANE:2 * LANE] + p[:, 2 * LANE:3 * LANE] + p[:, 3 * LANE:4 * LANE]
    hi = psum.astype(BF16)
    lo = (psum - hi.astype(F32)).astype(BF16)
    ov = ov_ref[:, 0:ncols]
    imp = _dot(ov, hi) + _dot(ov, lo)
    j = lax.broadcasted_iota(I32, imp.shape, 0).astype(F32)
    cur = ((t0 + lax.broadcasted_iota(I32, imp.shape, 1)) // L_SEL).astype(F32)
    forced = (j == 0.0) | (j == cur) | (j == cur - 1.0)
    score = jnp.where(j <= cur, jnp.where(forced, jnp.inf, imp), -jnp.inf)
    sel_mask = jnp.zeros(imp.shape, F32)
    for _ in range(min(N_SEL, nb)):
        mx = jnp.max(score, axis=0, keepdims=True)
        idx = jnp.min(jnp.where(score == mx, j, float(nb)), axis=0, keepdims=True)
        hit = j == idx
        sel_mask = jnp.where(hit, 1.0, sel_mask)
        score = jnp.where(hit, -jnp.inf, score)
    sel_mask = jnp.where(j <= cur, sel_mask, 0.0)
    mb_ref[0, g] = ((sel_mask - 1.0) * MASK_BIG).T.astype(BF16)
    per = tk // L_SEL
    ra = jnp.broadcast_to(jnp.max(sel_mask, axis=1, keepdims=True), sel_mask.shape)
    r = ra
    for sft in range(1, per):
        r = jnp.maximum(r, pltpu.roll(ra, nb - sft, 0))
    lane = lax.broadcasted_iota(I32, imp.shape, 1)
    fl = jnp.sum(jnp.where(j == (per * lane).astype(F32), r, 0.0), axis=0, keepdims=True)
    fl_ref[0, 0, g] = jnp.broadcast_to(fl, (8, LANE))


def _nsa_select(q_aug, kca, vct, gt, ov, *, nb, tk):
    b, s, qw = q_aug.shape
    ncp = kca.shape[2]
    nqb = s // Q_BLK
    n_var = 2 if (ncp // 2) % LANE == 0 and nqb % 2 == 0 else 1
    return pl.pallas_call(
        functools.partial(_nsa_select_kernel, nb=nb, tk=tk, n_var=n_var),
        grid=(b, nqb),
        in_specs=[
            pl.BlockSpec((1, Q_BLK, qw), lambda i, q: (i, q, 0)),
            pl.BlockSpec((1, N_KV, ncp, LANE), lambda i, q: (i, 0, 0, 0)),
            pl.BlockSpec((1, N_KV, DH, ncp), lambda i, q: (i, 0, 0, 0)),
            pl.BlockSpec((1, N_KV * 8, Q_BLK), lambda i, q: (i, 0, q)),
            _full(ov.shape),
        ],
        out_specs=[
            pl.BlockSpec((1, N_HEADS * DH, Q_BLK), lambda i, q: (i, 0, q)),
            pl.BlockSpec((1, N_KV, Q_BLK, nb), lambda i, q: (i, 0, q, 0)),
            pl.BlockSpec((1, 1, N_KV, 8, LANE), lambda i, q: (i, q, 0, 0, 0)),
        ],
        out_shape=[
            jax.ShapeDtypeStruct((b, N_HEADS * DH, s), F32),
            jax.ShapeDtypeStruct((b, N_KV, s, nb), BF16),
            jax.ShapeDtypeStruct((b, nqb, N_KV, 8, LANE), F32),
        ],
        compiler_params=_cparams(2),
        name="nsa_select",
    )(q_aug, kca, vct, gt, ov)


def _nsa_attend_kernel(bits_ref, q_ref, mb_ref, ks_ref, vs_ref, kw_ref, vw_ref, g1_ref, g2_ref, oc_ref, o_ref,
                       m_s, l_s, acc_s, p_s, pend_s, *, tk, n_win):
    bi, g, qb = pl.program_id(0), pl.program_id(1), pl.program_id(2)
    nqb = pl.num_programs(2)
    t0 = qb * Q_BLK
    qs = _stack_heads(q_ref[0])
    mb = mb_ref[0, 0]
    qfull = jnp.concatenate([qs, jnp.concatenate([mb] * HPG, axis=0)], axis=1)
    t_lane = t0 + (lax.broadcasted_iota(I32, (1, HPG * Q_BLK), 1) & (Q_BLK - 1))
    m_s[...] = jnp.full(m_s.shape, M_INIT, F32)
    l_s[...] = jnp.zeros_like(l_s)
    acc_s[...] = jnp.zeros_like(acc_s)
    p_s[...] = jnp.zeros_like(p_s)
    pend_s[0] = 0
    bits = bits_ref[(bi * N_KV + g) * nqb + qb]
    diag = (t0 + Q_BLK + tk - 1) // tk - 1

    def body(kt, carry):
        @pl.when(((bits >> kt) & 1) == 1)
        def _():
            k = ks_ref[0, 0, pl.ds(pl.multiple_of(kt * tk, tk), tk), :]
            s = _dot_nt(k, qfull)
            pv = _dot(vs_ref[0, 0, pend_s[0]], p_s[...])
            m_old = m_s[...]
            m_new = jnp.maximum(m_old, jnp.max(s, axis=0, keepdims=True))
            alpha = jnp.exp(m_old - m_new)
            p = jnp.exp(s - m_new)
            l_s[...] = alpha * l_s[...] + jnp.sum(p, axis=0, keepdims=True)
            acc_s[...] = alpha * (acc_s[...] + pv)
            m_s[...] = m_new
            p_s[...] = p.astype(BF16)
            pend_s[0] = kt
        return carry

    lax.fori_loop(0, diag, body, 0)
    k = ks_ref[0, 0, pl.ds(pl.multiple_of(diag * tk, tk), tk), :]
    s = _dot_nt(k, qfull)
    pv = _dot(vs_ref[0, 0, pend_s[0]], p_s[...])
    key = diag * tk + lax.broadcasted_iota(I32, s.shape, 0)
    s = jnp.where(key <= t_lane, s, NEG)
    m_old = m_s[...]
    m_new = jnp.maximum(m_old, jnp.max(s, axis=0, keepdims=True))
    alpha = jnp.exp(m_old - m_new)
    p = jnp.exp(s - m_new)
    l_fin = alpha * l_s[...] + jnp.sum(p, axis=0, keepdims=True)
    acc_fin = alpha * (acc_s[...] + pv) + _dot(vs_ref[0, 0, diag], p.astype(BF16))
    o_sel = acc_fin * (1.0 / l_fin)
    start = jnp.maximum(t0 - WINDOW, 0)
    kwin = kw_ref[0, 0, pl.ds(pl.multiple_of(start, LANE), n_win * LANE), :]
    sw = _dot_nt(kwin, qs)
    dist = t_lane - (start + lax.broadcasted_iota(I32, sw.shape, 0))
    sw = jnp.where((dist >= 0) & (dist < WINDOW), sw, NEG)
    mw = jnp.max(sw, axis=0, keepdims=True)
    pw = jnp.exp(sw - mw)
    lw = jnp.sum(pw, axis=0, keepdims=True)
    pwb = pw.astype(BF16)
    tile0 = start // LANE
    accw = None
    for i in range(n_win):
        d = _dot(vw_ref[0, 0, tile0 + i], pwb[i * LANE:(i + 1) * LANE])
        accw = d if accw is None else accw + d
    o_win = accw * (1.0 / lw)
    o_t = o_sel * _lane_heads(g1_ref[0]) + o_win * _lane_heads(g2_ref[0])
    o_all = _unstack_heads(o_t) + oc_ref[0]
    o_ref[0] = o_all.T.astype(BF16)


def _nsa_attend(bits, q_aug, mb, ksa, vst, kwa, vwt, gt, oc_t, *, tk):
    b, s, _ = q_aug.shape
    nb = mb.shape[-1]
    ka = ksa.shape[-1]
    nqb = s // Q_BLK
    n_win = WINDOW // LANE + 1
    grid_spec = pltpu.PrefetchScalarGridSpec(
        num_scalar_prefetch=1,
        grid=(b, N_KV, nqb),
        in_specs=[
            pl.BlockSpec((1, Q_BLK, HPG * LANE), lambda i, g, q, *_: (i, q, g)),
            pl.BlockSpec((1, 1, Q_BLK, nb), lambda i, g, q, *_: (i, g, q, 0)),
            pl.BlockSpec((1, 1, s, ka), lambda i, g, q, *_: (i, g, 0, 0)),
            pl.BlockSpec((1, 1, s // tk, DH, tk), lambda i, g, q, *_: (i, g, 0, 0, 0)),
            pl.BlockSpec((1, 1, s, LANE), lambda i, g, q, *_: (i, g, 0, 0)),
            pl.BlockSpec((1, 1, s // LANE, DH, LANE), lambda i, g, q, *_: (i, g, 0, 0, 0)),
            pl.BlockSpec((1, 8, Q_BLK), lambda i, g, q, *_: (i, N_KV + g, q)),
            pl.BlockSpec((1, 8, Q_BLK), lambda i, g, q, *_: (i, 2 * N_KV + g, q)),
            pl.BlockSpec((1, HPG * DH, Q_BLK), lambda i, g, q, *_: (i, g, q)),
        ],
        out_specs=pl.BlockSpec((1, Q_BLK, HPG * DH), lambda i, g, q, *_: (i, q, g)),
        scratch_shapes=[pltpu.VMEM((1, HPG * Q_BLK), F32), pltpu.VMEM((1, HPG * Q_BLK), F32),
                        pltpu.VMEM((DH, HPG * Q_BLK), F32), pltpu.VMEM((tk, HPG * Q_BLK), BF16),
                        pltpu.SMEM((1,), I32)],
    )
    return pl.pallas_call(
        functools.partial(_nsa_attend_kernel, tk=tk, n_win=n_win),
        grid_spec=grid_spec,
        out_shape=jax.ShapeDtypeStruct((b, s, N_HEADS * DH), BF16),
        compiler_params=_cparams(3),
        name="nsa_attend",
    )(bits, q_aug, mb, ksa, vst, kwa, vwt, gt, gt, oc_t)


def _lru_gates(xc, wa_ref, wx_ref, ba_ref, bx_ref, lam_ref):
    nt = wa_ref.shape[0]
    wdt = wa_ref.shape[1]
    pre_r, pre_i = [], []
    for i in range(nt):
        xs = xc[:, i * wdt:(i + 1) * wdt]
        pre_r.append(_mm(xs, wa_ref[i]))
        pre_i.append(_mm(xs, wx_ref[i]))
    r = jax.nn.sigmoid(jnp.concatenate(pre_r, axis=1) + ba_ref[...])
    ig = jax.nn.sigmoid(jnp.concatenate(pre_i, axis=1) + bx_ref[...])
    neg_lam = -lam_ref[...]
    softplus = jnp.maximum(neg_lam, 0.0) + jnp.log1p(jnp.exp(-jnp.abs(neg_lam)))
    log_a = -LRU_C * r * softplus
    a = jnp.exp(log_a)
    u = jnp.sqrt(1.0 - a * a) * (ig * xc)
    return a, u


def _rglru_seq_kernel(xr_ref, yr_ref, cw_ref, cb_ref, wa_ref, wx_ref, ba_ref, bx_ref, lam_ref,
                      o_ref, hl_ref, xbuf, a_s, u_s, h_s):
    ti = pl.program_id(1)
    tt = xr_ref.shape[1]

    @pl.when(ti == 0)
    def _():
        xbuf[0:8, :] = jnp.zeros((8, xbuf.shape[1]), F32)
        h_s[...] = jnp.zeros_like(h_s)

    x = xr_ref[0]
    xbuf[8:8 + tt, :] = x
    xc = cb_ref[...] + xbuf[5:5 + tt, :] * cw_ref[0:1, :]
    xc = xc + xbuf[6:6 + tt, :] * cw_ref[1:2, :]
    xc = xc + xbuf[7:7 + tt, :] * cw_ref[2:3, :]
    xc = xc + x * cw_ref[3:4, :]
    xbuf[0:8, :] = x[tt - 8:tt, :]
    a, u = _lru_gates(xc, wa_ref, wx_ref, ba_ref, bx_ref, lam_ref)
    sub = lax.broadcasted_iota(I32, a.shape, 0) & 7
    for dlt in (1, 2, 4):
        keep = sub >= dlt
        a_sh = jnp.where(keep, pltpu.roll(a, dlt, 0), 1.0)
        u_sh = jnp.where(keep, pltpu.roll(u, dlt, 0), 0.0)
        u = a * u_sh + u
        a = a * a_sh
    a_s[...] = a
    u_s[...] = u

    def grp(i, h):
        r0 = pl.multiple_of(i * 8, 8)
        hr = a_s[pl.ds(r0, 8), :] * h + u_s[pl.ds(r0, 8), :]
        u_s[pl.ds(r0, 8), :] = hr
        return jnp.broadcast_to(hr[7:8, :], hr.shape)

    h = lax.fori_loop(0, tt // 8, grp, h_s[...])
    h_s[...] = h
    o_ref[0] = (u_s[...] * jax.nn.gelu(yr_ref[0])).astype(o_ref.dtype)

    @pl.when(ti == pl.num_programs(1) - 1)
    def _():
        hl_ref[0] = h


def _rglru_seq(xr, yr, conv_w, conv_b, wa_t, wx_t, ba, bx, lam, *, tt):
    b, s, d = xr.shape
    row = lambda i, j: (i, j, 0)
    small = [conv_w, conv_b.reshape(1, d), wa_t, wx_t, ba.reshape(1, d), bx.reshape(1, d), lam.reshape(1, d)]
    return pl.pallas_call(
        _rglru_seq_kernel,
        grid=(b, s // tt),
        in_specs=[pl.BlockSpec((1, tt, d), row), pl.BlockSpec((1, tt, d), row)] + [_full(w.shape) for w in small],
        out_specs=[pl.BlockSpec((1, tt, d), row), pl.BlockSpec((1, 8, d), lambda i, j: (i, 0, 0))],
        out_shape=[jax.ShapeDtypeStruct((b, s, d), BF16), jax.ShapeDtypeStruct((b, 8, d), F32)],
        scratch_shapes=[pltpu.VMEM((tt + 8, d), F32), pltpu.VMEM((tt, d), F32), pltpu.VMEM((tt, d), F32),
                        pltpu.VMEM((8, d), F32)],
        compiler_params=_cparams(2),
        name="rglru_seq",
    )(xr, yr, *small)


def _mixer_out_kernel(x_ref, o_ref, r_ref, ga_ref, gr_ref, wa_ref, wr_ref, wo_ref, gp_ref, y_ref):
    att = _mm(o_ref[...], wa_ref[...])
    rec = _mm(r_ref[...], wr_ref[...])
    mix = jax.nn.sigmoid(ga_ref[...]) * att + jax.nn.sigmoid(gr_ref[...]) * rec
    y_ref[...] = x_ref[...] + _rms(_mm(mix, wo_ref[...]), gp_ref[...])


def _mixer_out(x, o, r, ga, gr, w_att, w_rnn, w_o, g_post, *, tm):
    n, d = x.shape
    spec = pl.BlockSpec((tm, d), lambda i: (i, 0))
    return pl.pallas_call(
        _mixer_out_kernel,
        grid=(n // tm,),
        in_specs=[spec] * 5 + [_full(w_att.shape), _full(w_rnn.shape), _full(w_o.shape), _full((1, d))],
        out_specs=spec,
        out_shape=jax.ShapeDtypeStruct((n, d), F32),
        compiler_params=_cparams(1),
        name="mixer_out",
    )(x, o, r, ga, gr, w_att, w_rnn, w_o, g_post.reshape(1, d))


def _sample_cmp_kernel(pt_ref, *refs, n_pg, n_c, t_pos):
    pages = refs[:n_pg]
    (qbd_ref, pea_ref, peb_ref, w1_ref, w2k_ref, w2v_ref, gsum_ref, ovt_ref, slope_ref,
     oc_ref, imp_ref, xs, ya, zb) = refs[n_pg:]
    st = pl.program_id(1)
    rows = n_pg * (PAGE // STRIDE)
    gw = N_KV * DH
    row = 2 * gw
    n_half = gw // LANE
    for k in range(n_pg):
        for s in range(2):
            x_t = pages[k][0, s].T
            for hf in range(n_half):
                xs[s, hf, k * PAGE:(k + 1) * PAGE, :] = x_t[:, hf * LANE:(hf + 1) * LANE]
    r0 = pl.multiple_of(st * rows, rows)
    for s in range(2):
        acc_a = None
        acc_b = None
        for l in range(STRIDE):
            x_l = jnp.concatenate([xs[s, hf, pl.ds(l, rows, stride=STRIDE), :] for hf in range(n_half)],
                                  axis=1).astype(BF16)
            da = _dot(x_l, w1_ref[s, 0, l])
            db = _dot(x_l, w1_ref[s, 1, l])
            acc_a = da if acc_a is None else acc_a + da
            acc_b = db if acc_b is None else acc_b + db
        ya[s, pl.ds(r0, rows), :] = acc_a
        zb[s, pl.ds(r0, rows), :] = acc_b

    @pl.when(st == pl.num_programs(1) - 1)
    def _():
        ncp = ya.shape[1]
        hid = []
        for s in range(2):
            sl = lambda l: slice(l * row + s * gw, l * row + (s + 1) * gw)
            bias = (_compress_sums(lambda l: pea_ref[:, sl(l)], w1_ref, s, 0)
                    + _compress_sums(lambda l: peb_ref[:, sl(l)], w1_ref, s, 1))[0:1]
            pre = ya[s] + pltpu.roll(zb[s], ncp - 1, 0) + bias
            hid.append(jax.nn.gelu(pre).astype(BF16))
        kc = _dot(hid[0], w2k_ref[...]).astype(BF16)
        vc = _dot(hid[1], w2v_ref[...]).astype(BF16)
        c_idx = lax.broadcasted_iota(I32, (1, ncp), 1)
        c_pos = c_idx * STRIDE + (L_CMP - 1)
        ok = (c_idx < n_c) & (c_pos <= t_pos)
        s_c = _dot_nt(qbd_ref[0], kc) + slope_ref[:, 0:1] * c_pos.astype(F32)
        s_c = jnp.where(ok, s_c, NEG)
        m = jnp.max(s_c, axis=1, keepdims=True)
        e = jnp.where(ok, jnp.exp(s_c - m), 0.0)
        l = jnp.sum(e, axis=1, keepdims=True)
        p = e * (1.0 / jnp.where(l > 0.0, l, 1.0))
        oc_ref[0] = _dot(p.astype(BF16), vc)
        p_hi = p.astype(BF16)
        p_lo = (p - p_hi.astype(F32)).astype(BF16)
        psum = _dot(gsum_ref[...], p_hi) + _dot(gsum_ref[...], p_lo)
        s_hi = psum.astype(BF16)
        s_lo = (psum - s_hi.astype(F32)).astype(BF16)
        imp_ref[0] = _dot(s_hi, ovt_ref[...]) + _dot(s_lo, ovt_ref[...])


def _sample_cmp(page_table, pool, qbd, consts, *, layer, n_pool, n_c, t_pos, n_pg):
    db, n_pages = page_table.shape
    steps = n_pages // n_pg
    ncp = n_pages * (PAGE // STRIDE)
    gw = pool.shape[2]
    nbp = consts[-2].shape[1]

    def page_spec(k):
        return pl.BlockSpec((1, 2, gw, PAGE), lambda i, st, pt: (layer * n_pool + pt[i, st * n_pg + k], 0, 0, 0))

    grid_spec = pltpu.PrefetchScalarGridSpec(
        num_scalar_prefetch=1,
        grid=(db, steps),
        in_specs=[page_spec(k) for k in range(n_pg)]
        + [pl.BlockSpec((1, N_HEADS, N_KV * DH), lambda i, st, pt: (i, 0, 0))]
        + [pl.BlockSpec(c.shape, functools.partial(lambda nd, *_: (0,) * nd, c.ndim)) for c in consts],
        out_specs=[pl.BlockSpec((1, N_HEADS, N_KV * DH), lambda i, st, pt: (i, 0, 0)),
                   pl.BlockSpec((1, 8, nbp), lambda i, st, pt: (i, 0, 0))],
        scratch_shapes=[pltpu.VMEM((2, gw // LANE, n_pg * PAGE, LANE), F32),
                        pltpu.VMEM((2, ncp, N_KV * DH), F32), pltpu.VMEM((2, ncp, N_KV * DH), F32)],
    )
    return pl.pallas_call(
        functools.partial(_sample_cmp_kernel, n_pg=n_pg, n_c=n_c, t_pos=t_pos),
        grid_spec=grid_spec,
        out_shape=[jax.ShapeDtypeStruct((db, N_HEADS, N_KV * DH), F32), jax.ShapeDtypeStruct((db, 8, nbp), F32)],
        compiler_params=pltpu.CompilerParams(dimension_semantics=("arbitrary", "arbitrary"),
                                             vmem_limit_bytes=56 * 1024 * 1024),
        name="sample_cmp",
    )(page_table, *([pool] * n_pg), qbd, *consts)


def _sample_topk_kernel(imp_ref, mb_ref, *, nbs, cur):
    imp = imp_ref[...]
    j = lax.broadcasted_iota(I32, imp.shape, 1).astype(F32)
    curf = float(cur)
    forced = (j == 0.0) | (j == curf) | (j == curf - 1.0)
    valid = (j <= curf) & (j < float(nbs))
    score = jnp.where(valid, jnp.where(forced, jnp.inf, imp), -jnp.inf)
    sel = jnp.zeros(imp.shape, F32)
    for _ in range(min(N_SEL, nbs)):
        mx = jnp.max(score, axis=1, keepdims=True)
        idx = jnp.min(jnp.where(score == mx, j, float(imp.shape[1])), axis=1, keepdims=True)
        hit = j == idx
        sel = jnp.where(hit, 1.0, sel)
        score = jnp.where(hit, -jnp.inf, score)
    sel = jnp.where(valid, sel, 0.0)
    mb_ref[...] = (sel - 1.0) * MASK_BIG


def _sample_topk(imp, *, nbs, cur):
    return pl.pallas_call(
        functools.partial(_sample_topk_kernel, nbs=nbs, cur=cur),
        grid=(1,),
        in_specs=[_full(imp.shape)],
        out_specs=_full(imp.shape),
        out_shape=jax.ShapeDtypeStruct(imp.shape, F32),
        compiler_params=_cparams(1),
        name="sample_topk",
    )(imp)


def _sample_attend_kernel(pt_ref, *refs, n_pg, past):
    pages = refs[:n_pg]
    (qbd_ref, mbh_ref, mbn_ref, ex_ref, ksn_ref, win_ref, kwn_ref, gate_ref, oc_ref, slope_ref,
     o_ref, wnew_ref, m_s, l_s, acc_s) = refs[n_pg:]
    st = pl.program_id(1)
    kvd = N_KV * DH
    qbd = qbd_ref[0]
    slope = slope_ref[:, 0:1]

    @pl.when(st == 0)
    def _():
        m_s[...] = jnp.full(m_s.shape, M_INIT, F32)
        l_s[...] = jnp.zeros_like(l_s)
        acc_s[...] = jnp.zeros_like(acc_s)

    kall = jnp.concatenate([pg[0, 0] for pg in pages], axis=1).astype(BF16)
    vall = jnp.concatenate([pg[0, 1] for pg in pages], axis=1).astype(BF16)
    nkey = n_pg * PAGE
    tok = st * nkey + lax.broadcasted_iota(I32, (1, nkey), 1)
    s = _dot(qbd, kall) + slope * tok.astype(F32) + _dot(mbh_ref[0, 0], ex_ref[...])
    s = jnp.where(tok <= past, s, NEG)
    m_old = m_s[...]
    m_new = jnp.maximum(m_old, jnp.max(s, axis=1, keepdims=True))
    alpha = jnp.exp(m_old - m_new)
    p = jnp.exp(s - m_new)
    l_s[...] = alpha * l_s[...] + jnp.sum(p, axis=1, keepdims=True)
    acc_s[...] = alpha * acc_s[...] + _dot_nt(p.astype(BF16), vall)
    m_s[...] = m_new

    @pl.when(st == pl.num_programs(1) - 1)
    def _():
        qf = qbd.astype(F32)
        t_bias = slope * float(past)
        ksn = ksn_ref[0]
        s_n = jnp.sum(qf * ksn[:, 0:kvd], axis=1, keepdims=True) + t_bias + mbn_ref[0][:, 0:1]
        m_old = m_s[...]
        m_new = jnp.maximum(m_old, s_n)
        alpha = jnp.exp(m_old - m_new)
        p_n = jnp.exp(s_n - m_new)
        l_fin = alpha * l_s[...] + p_n
        o_sel = (alpha * acc_s[...] + p_n * ksn[:, kvd:2 * kvd]) * (1.0 / l_fin)
        win = win_ref[0]
        n_win = win.shape[0]
        kwn = kwn_ref[0]
        r = lax.broadcasted_iota(I32, (1, n_win), 1)
        w_pos = past - n_win + r
        dist = past - w_pos
        s_w = _dot_nt(qbd, win[:, 0:kvd].astype(BF16)) + slope * w_pos.astype(F32)
        s_w = jnp.where((dist < WINDOW) & (w_pos >= 0), s_w, NEG)
        s_wn = jnp.sum(qf * kwn[:, 0:kvd], axis=1, keepdims=True) + t_bias
        m_w = jnp.maximum(jnp.max(s_w, axis=1, keepdims=True), s_wn)
        p_w = jnp.exp(s_w - m_w)
        p_wn = jnp.exp(s_wn - m_w)
        l_w = jnp.sum(p_w, axis=1, keepdims=True) + p_wn
        o_win = (_dot(p_w.astype(BF16), win[:, kvd:2 * kvd].astype(BF16)) + p_wn * kwn[:, kvd:2 * kvd]) * (1.0 / l_w)
        gts = jax.nn.sigmoid(gate_ref[0])
        o_wide = gts[:, 0:1] * oc_ref[0] + gts[:, 1:2] * o_sel + gts[:, 2:3] * o_win
        grp = lax.broadcasted_iota(I32, (N_HEADS, DH), 0) // HPG
        out = jnp.zeros((N_HEADS, DH), F32)
        for g in range(N_KV):
            out = out + jnp.where(grp == g, o_wide[:, g * DH:(g + 1) * DH], 0.0)
        o_ref[0] = out
        rows = lax.broadcasted_iota(I32, win.shape, 0)
        wnew_ref[0] = jnp.where(rows == n_win - 1, jnp.broadcast_to(kwn, win.shape), pltpu.roll(win, n_win - 1, 0))


def _sample_attend(page_table, pool, qbd, mbh, mbn, ex, ksn, win, kwn, gates, oc, slope, *, layer, n_pool, past, n_pg):
    db, n_pages = page_table.shape
    steps = n_pages // n_pg
    kv = win.shape[-1]
    n_win = win.shape[1]
    per_b = lambda shape: pl.BlockSpec((1,) + shape, lambda i, st, pt: (i,) + (0,) * len(shape))

    def page_spec(k):
        return pl.BlockSpec((1, 2, pool.shape[2], PAGE),
                            lambda i, st, pt: (layer * n_pool + pt[i, st * n_pg + k], 0, 0, 0))

    grid_spec = pltpu.PrefetchScalarGridSpec(
        num_scalar_prefetch=1,
        grid=(db, steps),
        in_specs=[page_spec(k) for k in range(n_pg)] + [
            per_b((N_HEADS, N_KV * DH)),
            pl.BlockSpec((1, 1, N_HEADS, LANE), lambda i, st, pt: (i, st, 0, 0)),
            per_b((N_HEADS, LANE)),
            pl.BlockSpec(ex.shape, lambda i, st, pt: (0, 0)),
            per_b((1, kv)), per_b((n_win, kv)), per_b((1, kv)), per_b((N_HEADS, LANE)),
            per_b((N_HEADS, N_KV * DH)),
            pl.BlockSpec(slope.shape, lambda i, st, pt: (0, 0)),
        ],
        out_specs=[per_b((N_HEADS, DH)), per_b((n_win, kv))],
        scratch_shapes=[pltpu.VMEM((N_HEADS, 1), F32), pltpu.VMEM((N_HEADS, 1), F32),
                        pltpu.VMEM((N_HEADS, N_KV * DH), F32)],
    )
    return pl.pallas_call(
        functools.partial(_sample_attend_kernel, n_pg=n_pg, past=past),
        grid_spec=grid_spec,
        out_shape=[jax.ShapeDtypeStruct((db, N_HEADS, DH), F32), jax.ShapeDtypeStruct((db, n_win, kv), F32)],
        compiler_params=_cparams(2),
        name="sample_attend",
    )(page_table, *([pool] * n_pg), qbd, mbh, mbn, ex, ksn, win, kwn, gates, oc, slope)


def _rglru_step_kernel(xr_ref, yr_ref, sc_ref, h0_ref, cw_ref, cb_ref, wa_ref, wx_ref, ba_ref, bx_ref, lam_ref,
                       o_ref, h_ref):
    xc = cb_ref[...] + sc_ref[0] * cw_ref[0:1, :]
    xc = xc + sc_ref[1] * cw_ref[1:2, :]
    xc = xc + sc_ref[2] * cw_ref[2:3, :]
    xc = xc + xr_ref[...] * cw_ref[3:4, :]
    a, u = _lru_gates(xc, wa_ref, wx_ref, ba_ref, bx_ref, lam_ref)
    h = a * h0_ref[...] + u
    h_ref[...] = h
    o_ref[...] = h * jax.nn.gelu(yr_ref[...])


def _rglru_step(xr, yr, sc, h0, conv_w, conv_b, wa_t, wx_t, ba, bx, lam):
    n, d = xr.shape
    args = [xr, yr, sc, h0, conv_w, conv_b.reshape(1, d), wa_t, wx_t, ba.reshape(1, d), bx.reshape(1, d),
            lam.reshape(1, d)]
    return pl.pallas_call(
        _rglru_step_kernel,
        grid=(1,),
        in_specs=[_full(a.shape) for a in args],
        out_specs=[_full((n, d)), _full((n, d))],
        out_shape=[jax.ShapeDtypeStruct((n, d), F32)] * 2,
        compiler_params=_cparams(1),
        name="rglru_step",
    )(*args)


def _slopes():
    h = jnp.arange(1, N_HEADS + 1, dtype=F32)
    return jnp.exp2(-8.0 * h / N_HEADS)


def _split3(x):
    p1 = x.astype(BF16).astype(F32)
    p2 = (x - p1).astype(BF16).astype(F32)
    p3 = (x - p1 - p2).astype(BF16).astype(F32)
    return p1, p2, p3


def _pos_cols(pos):
    hi = ((pos // L_SEL) * L_SEL).astype(F32)
    lo = (pos % L_SEL).astype(F32)
    return jnp.stack([hi, lo] * 3, axis=-1)


def _blockdiag(w, n):
    eye = jnp.eye(n, dtype=w.dtype)
    out = jnp.einsum("...ab,gh->...gahb", w, eye)
    return out.reshape(w.shape[:-2] + (n * w.shape[-2], n * w.shape[-1]))


def _lru_tiles(w):
    nb, c, _ = w.shape
    per = 4
    t = w.reshape(nb // per, per, c, c)
    eye = jnp.eye(per, dtype=w.dtype)
    return jnp.einsum("tpab,pq->tpaqb", t, eye).reshape(nb // per, per * c, per * c)


def _split_w_in(w_in):
    d = w_in.shape[0]
    sizes = (N_HEADS * DH, 2 * N_KV * DH, 2 * N_KV * DH, 2 * N_KV * DH, 3 * N_HEADS, d, d, d, d)
    out, o = [], 0
    for sz in sizes:
        out.append(w_in[:, o:o + sz])
        o += sz
    return out


def _prompt_layer(yp, cfg, p):
    b, s, d = yp.shape
    n = b * s
    tk = cfg["tk"]
    nb = s // L_SEL
    ka = LANE + nb
    n_c = (s - L_CMP) // STRIDE + 1
    ncp = s // STRIDE
    yp = _half_ffn(yp.reshape(n, d), p["f1_pre"], p["f1_post"], p["f1_wg"], p["f1_wu"], p["f1_wd"],
                   tm=cfg["ffn_tm"], tf=cfg["ffn_tf"]).reshape(b, s, d)
    (q_aug, kvc, kvc_bf, kvs, ksa, vst, kvw, kwa, vwt, gt) = _inproj_attn(
        yp, p["mix_pre"], p["attn_w"], p["pos_s"], p["pos_w"], tm=cfg["pa_tm"], tk=tk)
    xr, yr, ga, gr = _inproj_rnn(yp.reshape(n, d), p["mix_pre"], p["rnn_w"], tm=cfg["pr_tm"])
    kca, vct = _compress_prompt(kvc_bf.reshape(b, ncp, STRIDE * 2 * N_KV * DH), p["pe_a"], p["pe_b"], p["w1bd"], p["w2v"],
                                p["w2aug"], p["posc"], n_c=n_c)
    oc_t, mb, fl = _nsa_select(q_aug, kca, vct, gt, p["ov"], nb=nb, tk=tk)
    nt = s // tk
    flags = (fl[:, :, :, 0, :nt].transpose(0, 2, 1, 3) > 0.5).astype(I32)
    bits = jnp.sum(flags << jnp.arange(nt, dtype=I32), axis=-1).reshape(-1)
    o = _nsa_attend(bits, q_aug, mb, ksa, vst, kwa, vwt, gt, oc_t, tk=tk)
    rnn, h_last = _rglru_seq(xr.reshape(b, s, d), yr.reshape(b, s, d), p["conv_w"], p["conv_b"], p["wa_t"], p["wx_t"],
                             p["ba"], p["bx"], p["lam"], tt=cfg["lru_tt"])
    yp = _mixer_out(yp.reshape(n, d), o.reshape(n, d), rnn.reshape(n, d), ga, gr, p["w_att"], p["w_rnn"], p["w_o"],
                    p["mix_post"], tm=cfg["mo_tm"])
    yp = _half_ffn(yp, p["f2_pre"], p["f2_post"], p["f2_wg"], p["f2_wu"], p["f2_wd"],
                   tm=cfg["ffn_tm"], tf=cfg["ffn_tf"]).reshape(b, s, d)
    kv_shape = (b, s, 2, N_KV, DH)
    n_keep = min(WINDOW, s)
    state = (kvc.reshape(kv_shape), kvs.reshape(kv_shape), kvw.reshape(kv_shape)[:, s - n_keep:],
             xr.reshape(b, s, d)[:, s - (CONV_W - 1):], h_last[:, 0])
    return yp, state


def _prep_prompt_params(l, s, w):
    d = w["w_in"].shape[1]
    nb = s // L_SEL
    ka = LANE + nb
    ncp = s // STRIDE
    wq, wkc, wks, wkw, wg, wxr, wyr, wga, wgr = _split_w_in(w["w_in"][l])
    zeros = lambda *sh: jnp.zeros(sh, F32)
    wq_aug = jnp.concatenate([(wq * SCALE).reshape(d, N_HEADS, DH), zeros(d, N_HEADS, LANE - DH)], axis=-1)
    wq_aug = wq_aug.reshape(d, N_HEADS * LANE).astype(BF16)
    sl = jnp.stack([c for piece in _split3(_slopes()) for c in (piece, piece)], axis=-1)
    qbias = jnp.concatenate([zeros(N_HEADS, DH), sl, zeros(N_HEADS, LANE - DH - POS_COLS)], axis=-1)
    qbias = qbias.reshape(1, N_HEADS * LANE)
    kpart = lambda wk: wk[:, :N_KV * DH].reshape(d, N_KV, DH)
    wks_aug = jnp.concatenate([kpart(wks), zeros(d, N_KV, ka - DH)], axis=-1).reshape(d, N_KV * ka).astype(BF16)
    wkw_aug = jnp.concatenate([kpart(wkw), zeros(d, N_KV, LANE - DH)], axis=-1).reshape(d, N_KV * LANE).astype(BF16)
    wg4 = wg.reshape(d, 3, N_KV, HPG)
    wg_pad = jnp.concatenate([wg4, zeros(d, 3, N_KV, 8 - HPG)], axis=-1).reshape(d, 3 * N_KV * 8)
    wg_pad = jnp.concatenate([wg_pad, zeros(d, LANE - 3 * N_KV * 8)], axis=-1).astype(BF16)
    tok = jnp.arange(s, dtype=I32)
    pos6 = _pos_cols(tok)
    onehot = (tok[:, None] // L_SEL == jnp.arange(nb, dtype=I32)[None, :]).astype(F32)
    pos_s = jnp.concatenate([zeros(s, DH), pos6, zeros(s, LANE - DH - POS_COLS), onehot], axis=-1)
    pos_w = jnp.concatenate([zeros(s, DH), pos6, zeros(s, LANE - DH - POS_COLS)], axis=-1)
    cpos = jnp.arange(ncp, dtype=I32) * STRIDE + (L_CMP - 1)
    posc = jnp.concatenate([zeros(ncp, DH), _pos_cols(cpos), zeros(ncp, LANE - DH - POS_COLS)], axis=-1)
    w1 = w["cmp_w1"][l]
    w1bd = _blockdiag(w1, N_KV).reshape(2, 2, STRIDE, N_KV * DH, N_KV * DH).astype(BF16)
    w2 = w["cmp_w2"][l]
    w2bd = _blockdiag(w2, N_KV).astype(BF16)
    eye = jnp.eye(N_KV, dtype=F32)
    w2aug = jnp.einsum("de,gh->ghde", w2[0], eye).reshape(N_KV, N_KV * DH, DH)
    w2aug = jnp.concatenate([w2aug, zeros(N_KV, N_KV * DH, LANE - DH)], axis=-1).astype(BF16)
    pe = w["cmp_pos"][l]
    pe_rows = jnp.broadcast_to(pe[:, :, None, :], (L_CMP, 2, N_KV, DH)).reshape(2, STRIDE * 2 * N_KV * DH)
    pe_a = jnp.broadcast_to(pe_rows[0:1], (16, pe_rows.shape[1])).astype(BF16)
    pe_b = jnp.broadcast_to(pe_rows[1:2], (16, pe_rows.shape[1])).astype(BF16)
    c_idx = jnp.arange(ncp, dtype=I32)[None, :] * STRIDE
    j_idx = jnp.arange(nb, dtype=I32)[:, None]
    ov = ((c_idx < (j_idx + 1) * L_SEL) & (c_idx + L_CMP > j_idx * L_SEL)).astype(BF16)
    return {
        "f1_pre": w["ffn1_norm_pre"][l], "f1_post": w["ffn1_norm_post"][l],
        "f1_wg": w["ffn1_w_gate"][l].astype(BF16), "f1_wu": w["ffn1_w_up"][l].astype(BF16),
        "f1_wd": w["ffn1_w_down"][l].astype(BF16),
        "f2_pre": w["ffn2_norm_pre"][l], "f2_post": w["ffn2_norm_post"][l],
        "f2_wg": w["ffn2_w_gate"][l].astype(BF16), "f2_wu": w["ffn2_w_up"][l].astype(BF16),
        "f2_wd": w["ffn2_w_down"][l].astype(BF16),
        "mix_pre": w["mix_norm_pre"][l], "mix_post": w["mix_norm_post"][l],
        "attn_w": [wq_aug, qbias, wkc.astype(BF16), wks.astype(BF16), wks_aug, wkw.astype(BF16), wkw_aug, wg_pad],
        "rnn_w": jnp.concatenate([wxr, wyr, wga, wgr], axis=1).astype(BF16),
        "pos_s": pos_s, "pos_w": pos_w, "posc": posc, "pe_a": pe_a, "pe_b": pe_b, "w1bd": w1bd, "w2v": w2bd[1], "w2aug": w2aug,
        "ov": ov,
        "conv_w": w["conv_w"][l], "conv_b": w["conv_b"][l],
        "wa_t": _lru_tiles(w["lru_wa"][l]).astype(BF16), "wx_t": _lru_tiles(w["lru_wx"][l]).astype(BF16),
        "ba": w["lru_ba"][l], "bx": w["lru_bx"][l], "lam": w["lru_lambda"][l],
        "w_att": w["w_attn_out"][l].astype(BF16), "w_rnn": w["w_rnn_out"][l].astype(BF16),
        "w_o": w["w_o"][l].astype(BF16),
    }


def _page_major(cache):
    nl, n_pool, page, _, ng, dh = cache.shape
    return cache.transpose(0, 1, 3, 4, 5, 2).reshape(nl * n_pool, 2, ng * dh, page)


def _sample_layer(ys, l, w, cache_cmp, cache_slc, win_buf, state_conv, state_h, page_table):
    db, d = ys.shape
    n_pages = page_table.shape[1]
    past = n_pages * PAGE
    n_pool = cache_cmp.shape[1]
    n_c = (past + 1 - L_CMP) // STRIDE + 1
    assert (n_c - 1) * STRIDE + L_CMP <= past, "compressed blocks must lie inside the paged history"
    assert past % 1024 == 0 and past >= WINDOW and past // L_SEL + 1 <= 256
    n_win = win_buf.shape[1]
    nbs = -(-(past + 1) // L_SEL)
    cur = past // L_SEL
    nbp = 256
    ncp = past // STRIDE
    ys = _half_ffn(ys, w["ffn1_norm_pre"][l], w["ffn1_norm_post"][l], w["ffn1_w_gate"][l], w["ffn1_w_up"][l],
                   w["ffn1_w_down"][l], tm=db, tf=256)
    w_in = w["w_in"][l]
    d_in = w_in.shape[1]
    pad = (-d_in) % 256
    proj = _norm_proj(ys, w["mix_norm_pre"][l], jnp.pad(w_in, ((0, 0), (0, pad))), tn=256)
    sizes = (N_HEADS * DH, 2 * N_KV * DH, 2 * N_KV * DH, 2 * N_KV * DH, 3 * N_HEADS, d, d, d, d)
    parts, o = [], 0
    for sz in sizes:
        parts.append(proj[:, o:o + sz])
        o += sz
    q, kvc, kvs, kvw, g_nsa, xr, yr, ga, gr = parts
    onehot = (jnp.arange(N_HEADS)[:, None] // HPG == jnp.arange(N_KV)[None, :]).astype(F32)
    qbd = ((q * SCALE).reshape(db, N_HEADS, 1, DH) * onehot[None, :, :, None]).reshape(db, N_HEADS, N_KV * DH)
    qbd = qbd.astype(BF16)
    slope = jnp.broadcast_to(_slopes()[:, None], (N_HEADS, LANE))
    w1 = w["cmp_w1"][l]
    w1bd = _blockdiag(w1, N_KV).reshape(2, 2, STRIDE, N_KV * DH, N_KV * DH).astype(BF16)
    w2bd = _blockdiag(w["cmp_w2"][l], N_KV).astype(BF16)
    pe = w["cmp_pos"][l]
    pe_rows = jnp.broadcast_to(pe[:, :, None, :], (L_CMP, 2, N_KV, DH)).reshape(2, STRIDE * 2 * N_KV * DH)
    pe_a = jnp.broadcast_to(pe_rows[0:1], (16, pe_rows.shape[1])).astype(BF16)
    pe_b = jnp.broadcast_to(pe_rows[1:2], (16, pe_rows.shape[1])).astype(BF16)
    gsum = (jnp.arange(8)[:, None] == jnp.arange(N_HEADS)[None, :] // HPG).astype(BF16)
    c_idx = jnp.arange(ncp, dtype=I32)[:, None] * STRIDE
    j_idx = jnp.arange(nbp, dtype=I32)[None, :]
    ovt = ((c_idx < (j_idx + 1) * L_SEL) & (c_idx + L_CMP > j_idx * L_SEL) & (j_idx < nbs)).astype(BF16)
    pool_c = _page_major(cache_cmp)
    oc, imp = _sample_cmp(page_table, pool_c, qbd, [pe_a, pe_b, w1bd, w2bd[0], w2bd[1], gsum, ovt, slope],
                          layer=l, n_pool=n_pool, n_c=n_c, t_pos=past, n_pg=min(32, n_pages))
    mb = _sample_topk(imp.reshape(db * 8, nbp), nbs=nbs, cur=cur).reshape(db, 8, nbp)[:, :N_KV]
    n_pg = 8
    steps = n_pages // n_pg
    per_step = n_pg * PAGE // L_SEL
    mb_h = jnp.repeat(mb, HPG, axis=1)
    mbh = mb_h[:, :, :steps * per_step].reshape(db, N_HEADS, steps, per_step).transpose(0, 2, 1, 3)
    mbh = jnp.pad(mbh, ((0, 0), (0, 0), (0, 0), (0, LANE - per_step))).astype(BF16)
    mbn = jnp.pad(mb_h[:, :, cur:cur + 1], ((0, 0), (0, 0), (0, LANE - 1)))
    ex = (jnp.arange(LANE, dtype=I32)[:, None] == jnp.arange(n_pg * PAGE, dtype=I32)[None, :] // L_SEL).astype(BF16)
    gates = jnp.pad(g_nsa.reshape(db, 3, N_HEADS).transpose(0, 2, 1), ((0, 0), (0, 0), (0, LANE - 3)))
    pool_s = _page_major(cache_slc)
    o_att, win_new = _sample_attend(page_table, pool_s, qbd, mbh, mbn, ex, kvs.reshape(db, 1, -1),
                                    win_buf.reshape(db, n_win, -1), kvw.reshape(db, 1, -1), gates, oc, slope,
                                    layer=l, n_pool=n_pool, past=past, n_pg=n_pg)
    rnn, h_new = _rglru_step(xr, yr, state_conv.transpose(1, 0, 2), state_h, w["conv_w"][l], w["conv_b"][l],
                             _lru_tiles(w["lru_wa"][l]), _lru_tiles(w["lru_wx"][l]), w["lru_ba"][l], w["lru_bx"][l],
                             w["lru_lambda"][l])
    ys = _mixer_out(ys, o_att.reshape(db, N_HEADS * DH), rnn, ga, gr, w["w_attn_out"][l], w["w_rnn_out"][l],
                    w["w_o"][l], w["mix_norm_post"][l], tm=db)
    ys = _half_ffn(ys, w["ffn2_norm_pre"][l], w["ffn2_norm_post"][l], w["ffn2_w_gate"][l], w["ffn2_w_up"][l],
                   w["ffn2_w_down"][l], tm=db, tf=256)
    kv_shape = (db, 1, 2, N_KV, DH)
    conv_new = jnp.concatenate([state_conv[:, 1:], xr[:, None, :]], axis=1)
    state = (kvc.reshape(kv_shape), kvs.reshape(kv_shape), win_new.reshape(db, n_win, 2, N_KV, DH), conv_new, h_new)
    return ys, state


def _prompt_cfg(s):
    return {"tk": 256, "ffn_tm": 512, "ffn_tf": 1408, "pa_tm": 256, "pr_tm": 512, "lru_tt": 512, "mo_tm": 512}


def kernel(x_prompt, x_sample, cache_cmp_kv, cache_slc_kv, cache_win_kv, state_conv, state_h, page_table,
           ffn1_norm_pre, ffn1_norm_post, ffn1_w_gate, ffn1_w_up, ffn1_w_down,
           mix_norm_pre, mix_norm_post, w_in, cmp_w1, cmp_w2, cmp_pos, conv_w, conv_b,
           lru_wa, lru_ba, lru_wx, lru_bx, lru_lambda, w_attn_out, w_rnn_out, w_o,
           ffn2_norm_pre, ffn2_norm_post, ffn2_w_gate, ffn2_w_up, ffn2_w_down):
    w = dict(ffn1_norm_pre=ffn1_norm_pre, ffn1_norm_post=ffn1_norm_post, ffn1_w_gate=ffn1_w_gate,
             ffn1_w_up=ffn1_w_up, ffn1_w_down=ffn1_w_down, mix_norm_pre=mix_norm_pre, mix_norm_post=mix_norm_post,
             w_in=w_in, cmp_w1=cmp_w1, cmp_w2=cmp_w2, cmp_pos=cmp_pos, conv_w=conv_w, conv_b=conv_b,
             lru_wa=lru_wa, lru_ba=lru_ba, lru_wx=lru_wx, lru_bx=lru_bx, lru_lambda=lru_lambda,
             w_attn_out=w_attn_out, w_rnn_out=w_rnn_out, w_o=w_o, ffn2_norm_pre=ffn2_norm_pre,
             ffn2_norm_post=ffn2_norm_post, ffn2_w_gate=ffn2_w_gate, ffn2_w_up=ffn2_w_up, ffn2_w_down=ffn2_w_down)
    depth = w_in.shape[0]
    b, s, d = x_prompt.shape
    assert s % 512 == 0 and s >= WINDOW + Q_BLK and s // L_SEL + LANE <= 256
    yp = x_prompt
    p_states = []
    for l in range(depth):
        yp, st = _prompt_layer(yp, _prompt_cfg(s), _prep_prompt_params(l, s, w))
        p_states.append(st)
    p_out = [jnp.stack([st[i] for st in p_states]) for i in range(5)]
    db = x_sample.shape[0]
    assert x_sample.shape[1] == 1, "one new token per running sequence"
    ys = x_sample.reshape(db, d)
    s_states = []
    for l in range(depth):
        ys, st = _sample_layer(ys, l, w, cache_cmp_kv, cache_slc_kv, cache_win_kv[l], state_conv[l], state_h[l],
                               page_table)
        s_states.append(st)
    s_out = [jnp.stack([st[i] for st in s_states]) for i in range(5)]
    return (yp, ys.reshape(db, 1, d), *p_out, *s_out)
```
